```python
import jax, jax.numpy as jnp
from jax import lax
import numpy as np

D_MODEL = 1024
BATCH = 2
SEQ = 8192
DEPTH = 4

CTX_LEN = 256
GRID_W = 64

BRANCH_W = D_MODEL // 2
N_BRANCH = 3
HEAD_DIM = 64
N_HEADS = BRANCH_W // HEAD_DIM
N_KV_HEADS = N_HEADS // 4
ATTN_W = N_HEADS * HEAD_DIM
KV_W = N_KV_HEADS * HEAD_DIM
WINDOW = 128
BLOCK = 128
ROPE_BASE = 10000.0
CONV_K = 31
LRU_BLOCKS = 8
LRU_BLOCK_DIM = BRANCH_W // LRU_BLOCKS
LRU_CONV_K = 4
LRU_C = 8.0
IN_SIZES = (ATTN_W, KV_W, KV_W, BRANCH_W, 2 * BRANCH_W, BRANCH_W, BRANCH_W, BRANCH_W)
IN_COLS = sum(IN_SIZES)
DEEPNORM_ALPHA = (2 * DEPTH) ** 0.25
DEEPNORM_BETA = (8 * DEPTH) ** -0.25
LN_EPS = 1e-6

kernel_name = "hybrid_gqa_conformer_rglru_diffusion"


def layer_norm(x):
    xf = x.astype(jnp.float32)
    mu = jnp.mean(xf, -1, keepdims=True)
    var = jnp.mean(jnp.square(xf - mu), -1, keepdims=True)
    return ((xf - mu) * lax.rsqrt(var + LN_EPS)).astype(x.dtype)


def split_in_proj(p):
    idx = np.cumsum(IN_SIZES)[:-1].tolist()
    return jnp.split(p, idx, axis=-1)


def axial_rope_tables(rows, dtype):
    quarter = HEAD_DIM // 4
    inv = ROPE_BASE ** (-jnp.arange(quarter, dtype=jnp.float32) / quarter)
    r = jnp.arange(rows, dtype=jnp.float32)
    col = jnp.arange(GRID_W, dtype=jnp.float32)
    n = rows * GRID_W
    ang_r = jnp.broadcast_to(r[:, None, None] * inv, (rows, GRID_W, quarter)).reshape(n, quarter)
    ang_c = jnp.broadcast_to(col[None, :, None] * inv, (rows, GRID_W, quarter)).reshape(n, quarter)
    ang = jnp.concatenate([ang_r, ang_r, ang_c, ang_c], -1)
    return jnp.cos(ang).astype(dtype), jnp.sin(ang).astype(dtype)


def apply_axial_rope(x, cos, sin):
    x1, x2, x3, x4 = jnp.split(x, 4, axis=-1)
    rot = jnp.concatenate([-x2, x1, -x4, x3], -1)
    return x * cos[:, None, :] + rot * sin[:, None, :]


def window_attention(q, k, v, k_ctx, v_ctx, sink):
    B, S = q.shape[:2]
    C = k_ctx.shape[1]
    nb = S // BLOCK
    G = N_HEADS // N_KV_HEADS
    scale = HEAD_DIM ** -0.5
    qb = q.reshape(B, nb, BLOCK, N_KV_HEADS, G, HEAD_DIM)
    pad = ((0, 0), (BLOCK, BLOCK), (0, 0), (0, 0))
    kp = jnp.pad(k, pad).reshape(B, nb + 2, BLOCK, N_KV_HEADS, HEAD_DIM)
    vp = jnp.pad(v, pad).reshape(B, nb + 2, BLOCK, N_KV_HEADS, HEAD_DIM)
    kb = jnp.concatenate([kp[:, :-2], kp[:, 1:-1], kp[:, 2:]], axis=2)
    vb = jnp.concatenate([vp[:, :-2], vp[:, 1:-1], vp[:, 2:]], axis=2)
    s_loc = jnp.einsum('bnqhgd,bnjhd->bnhgqj', qb, kb).astype(jnp.float32) * scale
    qpos = jnp.arange(S).reshape(nb, BLOCK)
    kpos = (jnp.arange(nb)[:, None] - 1) * BLOCK + jnp.arange(3 * BLOCK)[None, :]
    diff = kpos[:, None, :] - qpos[:, :, None]
    valid = (jnp.abs(diff) <= WINDOW) & (kpos[:, None, :] >= 0) & (kpos[:, None, :] < S)
    s_loc = jnp.where(valid[None, :, None, None], s_loc, -jnp.inf)
    s_ctx = jnp.einsum('bnqhgd,bchd->bnhgqc', qb, k_ctx).astype(jnp.float32) * scale
    s_sink = jnp.broadcast_to(sink.reshape(N_KV_HEADS, G)[None, None, :, :, None, None].astype(jnp.float32),
                              s_loc.shape[:-1] + (1,))
    p = jax.nn.softmax(jnp.concatenate([s_loc, s_ctx, s_sink], -1), axis=-1)
    p_loc = p[..., :3 * BLOCK].astype(v.dtype)
    p_ctx = p[..., 3 * BLOCK:3 * BLOCK + C].astype(v.dtype)
    out = jnp.einsum('bnhgqj,bnjhd->bnqhgd', p_loc, vb) + jnp.einsum('bnhgqc,bchd->bnqhgd', p_ctx, v_ctx)
    return out.reshape(B, S, ATTN_W)


def context_attention(q, k, v, sink):
    B, C = q.shape[:2]
    G = N_HEADS // N_KV_HEADS
    qg = q.reshape(B, C, N_KV_HEADS, G, HEAD_DIM)
    s = jnp.einsum('bqhgd,bchd->bhgqc', qg, k).astype(jnp.float32) * (HEAD_DIM ** -0.5)
    s_sink = jnp.broadcast_to(sink.reshape(N_KV_HEADS, G)[None, :, :, None, None].astype(jnp.float32),
                              s.shape[:-1] + (1,))
    p = jax.nn.softmax(jnp.concatenate([s, s_sink], -1), axis=-1)[..., :C].astype(v.dtype)
    return jnp.einsum('bhgqc,bchd->bqhgd', p, v).reshape(B, C, ATTN_W)


def depthwise_conv(x, w, b, pad):
    C = x.shape[-1]
    y = lax.conv_general_dilated(x, w[:, None, :].astype(x.dtype), window_strides=(1,), padding=[pad],
                                 dimension_numbers=('NWC', 'WIO', 'NWC'), feature_group_count=C)
    return y + b.astype(x.dtype)


def conformer_conv(u, dw_w, dw_b, ln_g, ln_b):
    a, g = jnp.split(u, 2, axis=-1)
    y = a * jax.nn.sigmoid(g)
    y = depthwise_conv(y, dw_w, dw_b, (CONV_K // 2, CONV_K // 2))
    y = layer_norm(y) * ln_g + ln_b
    return jax.nn.silu(y)


def block_diag_linear(x, w, b):
    xb = x.reshape(x.shape[:-1] + (LRU_BLOCKS, LRU_BLOCK_DIM))
    return jnp.einsum('btnd,nde->btne', xb, w).reshape(x.shape) + b


def linear_scan(a, u, h0):
    def combine(left, right):
        a_l, u_l = left
        a_r, u_r = right
        return a_l * a_r, a_r * u_l + u_r
    a_cum, h = lax.associative_scan(combine, (a, u), axis=1)
    return h + a_cum * h0[:, None, :]


def rglru_inputs(x, w_a, b_a, w_x, b_x, lam):
    r = jax.nn.sigmoid(block_diag_linear(x, w_a, b_a)).astype(jnp.float32)
    i = jax.nn.sigmoid(block_diag_linear(x, w_x, b_x)).astype(jnp.float32)
    log_a = -LRU_C * r * jax.nn.softplus(-lam.astype(jnp.float32))
    a = jnp.exp(log_a)
    u = jnp.sqrt(-jnp.expm1(2.0 * log_a)) * i * x.astype(jnp.float32)
    return a, u


def rglru_direction(x_ctx, x_lat, conv_w, conv_b, w_a, b_a, w_x, b_x, lam, reverse):
    pad = (0, LRU_CONV_K - 1) if reverse else (LRU_CONV_K - 1, 0)
    xc = depthwise_conv(x_ctx, conv_w, conv_b, pad)
    xl = depthwise_conv(x_lat, conv_w, conv_b, pad)
    if reverse:
        xc, xl = xc[:, ::-1], xl[:, ::-1]
    a_c, u_c = rglru_inputs(xc, w_a, b_a, w_x, b_x, lam)
    h_c = linear_scan(a_c, u_c, jnp.zeros((xc.shape[0], xc.shape[-1]), jnp.float32))
    a_l, u_l = rglru_inputs(xl, w_a, b_a, w_x, b_x, lam)
    h_l = linear_scan(a_l, u_l, h_c[:, -1])
    if reverse:
        h_c, h_l = h_c[:, ::-1], h_l[:, ::-1]
    return h_c, h_l


def merge_branches(h, ys, gs, w_branch, w_gate, b_gate, w_out):
    y = jnp.stack([yb.astype(h.dtype) * jax.nn.silu(gb) for yb, gb in zip(ys, gs)], axis=-2)
    proj = jnp.einsum('btnw,nwd->btnd', y, w_branch)
    g = jax.nn.sigmoid(h @ w_gate + b_gate).reshape(h.shape[:-1] + (N_BRANCH, D_MODEL))
    return jnp.sum(proj * g, axis=-2) @ w_out


def deepnorm_post(x, out, g, b):
    return layer_norm(DEEPNORM_ALPHA * x + out) * g + b


def setup_inputs(seed: int = 0) -> dict:
    key = jax.random.key(seed)
    ks = jax.random.split(key, 26)
    f32 = jnp.float32
    D = D_MODEL

    def nrm(k, shape, s):
        return jax.random.normal(k, shape, f32) * s

    u = jax.random.uniform(ks[18], (DEPTH, 2, BRANCH_W), f32, 0.9, 0.999)
    a0 = u ** (1.0 / LRU_C)
    lam = jnp.log(a0) - jnp.log1p(-a0)
    return {
        "x": nrm(ks[0], (BATCH, SEQ, D), 1.0),
        "c": nrm(ks[1], (BATCH, D), 1.0),
        "ctx": nrm(ks[2], (BATCH, CTX_LEN, D), 1.0),
        "c_ctx": nrm(ks[3], (D,), 1.0),
        "w_ada": nrm(ks[4], (DEPTH, D, 3 * D), D ** -0.5),
        "b_ada": nrm(ks[5], (DEPTH, 3 * D), 0.02),
        "w_in": nrm(ks[6], (DEPTH, D, IN_COLS), D ** -0.5),
        "attn_sink": nrm(ks[7], (DEPTH, N_HEADS), 0.5),
        "conv_dw_w": nrm(ks[8], (DEPTH, CONV_K, BRANCH_W), CONV_K ** -0.5),
        "conv_dw_b": nrm(ks[9], (DEPTH, BRANCH_W), 0.02),
        "conv_ln_g": 1.0 + nrm(ks[10], (DEPTH, BRANCH_W), 0.02),
        "conv_ln_b": nrm(ks[11], (DEPTH, BRANCH_W), 0.02),
        "lru_conv_w": nrm(ks[12], (DEPTH, 2, LRU_CONV_K, BRANCH_W), LRU_CONV_K ** -0.5),
        "lru_conv_b": nrm(ks[13], (DEPTH, 2, BRANCH_W), 0.02),
        "lru_w_a": nrm(ks[14], (DEPTH, 2, LRU_BLOCKS, LRU_BLOCK_DIM, LRU_BLOCK_DIM), LRU_BLOCK_DIM ** -0.5),
        "lru_b_a": nrm(ks[15], (DEPTH, 2, BRANCH_W), 0.02),
        "lru_w_x": nrm(ks[16], (DEPTH, 2, LRU_BLOCKS, LRU_BLOCK_DIM, LRU_BLOCK_DIM), LRU_BLOCK_DIM ** -0.5),
        "lru_b_x": nrm(ks[17], (DEPTH, 2, BRANCH_W), 0.02),
        "lru_lambda": lam,
        "w_branch": nrm(ks[19], (DEPTH, N_BRANCH, BRANCH_W, D), BRANCH_W ** -0.5 * DEEPNORM_BETA),
        "w_gate": nrm(ks[20], (DEPTH, D, N_BRANCH * D), D ** -0.5),
        "b_gate": nrm(ks[21], (DEPTH, N_BRANCH * D), 0.02),
        "w_out": nrm(ks[22], (DEPTH, D, D), D ** -0.5 * DEEPNORM_BETA),
        "ln_g": 1.0 + nrm(ks[23], (DEPTH, D), 0.02),
        "ln_b": nrm(ks[24], (DEPTH, D), 0.02),
    }


def reference(x, c, ctx, c_ctx, w_ada, b_ada, w_in, attn_sink, conv_dw_w, conv_dw_b, conv_ln_g, conv_ln_b,
              lru_conv_w, lru_conv_b, lru_w_a, lru_b_a, lru_w_x, lru_b_x, lru_lambda,
              w_branch, w_gate, b_gate, w_out, ln_g, ln_b):
    B, S = x.shape[:2]
    C = ctx.shape[1]
    rows = S // GRID_W
    cos, sin = axial_rope_tables(rows, x.dtype)
    for l in range(DEPTH):
        last = l == DEPTH - 1
        shift, scale, gate = jnp.split(jax.nn.silu(c) @ w_ada[l] + b_ada[l], 3, axis=-1)
        shift_c, scale_c, gate_c = jnp.split(jax.nn.silu(c_ctx) @ w_ada[l] + b_ada[l], 3, axis=-1)
        h = layer_norm(x) * (1.0 + scale[:, None]) + shift[:, None]
        hc = layer_norm(ctx) * (1.0 + scale_c) + shift_c
        q, k, v, g_att, u_conv, g_conv, x_lru, g_lru = split_in_proj(h @ w_in[l])
        qc, kc, vc, g_att_c, u_conv_c, g_conv_c, x_lru_c, g_lru_c = split_in_proj(hc @ w_in[l])
        q = apply_axial_rope(q.reshape(B, S, N_HEADS, HEAD_DIM), cos, sin)
        k = apply_axial_rope(k.reshape(B, S, N_KV_HEADS, HEAD_DIM), cos, sin)
        v = v.reshape(B, S, N_KV_HEADS, HEAD_DIM)
        kc = kc.reshape(B, C, N_KV_HEADS, HEAD_DIM)
        vc = vc.reshape(B, C, N_KV_HEADS, HEAD_DIM)
        y_att = window_attention(q, k, v, kc, vc, attn_sink[l])
        y_conv = conformer_conv(u_conv, conv_dw_w[l], conv_dw_b[l], conv_ln_g[l], conv_ln_b[l])
        hf_c, hf_l = rglru_direction(x_lru_c, x_lru, lru_conv_w[l, 0], lru_conv_b[l, 0], lru_w_a[l, 0],
                                     lru_b_a[l, 0], lru_w_x[l, 0], lru_b_x[l, 0], lru_lambda[l, 0], False)
        hb_c, hb_l = rglru_direction(x_lru_c, x_lru, lru_conv_w[l, 1], lru_conv_b[l, 1], lru_w_a[l, 1],
                                     lru_b_a[l, 1], lru_w_x[l, 1], lru_b_x[l, 1], lru_lambda[l, 1], True)
        y_lru = hf_l + hb_l
        out = merge_branches(h, (y_att, y_conv, y_lru), (g_att, g_conv, g_lru),
                             w_branch[l], w_gate[l], b_gate[l], w_out[l])
        x_new = deepnorm_post(x, gate[:, None] * out, ln_g[l], ln_b[l])
        if not last:
            y_att_c = context_attention(qc.reshape(B, C, N_HEADS, HEAD_DIM), kc, vc, attn_sink[l])
            y_conv_c = conformer_conv(u_conv_c, conv_dw_w[l], conv_dw_b[l], conv_ln_g[l], conv_ln_b[l])
            y_lru_c = hf_c + hb_c
            out_c = merge_branches(hc, (y_att_c, y_conv_c, y_lru_c), (g_att_c, g_conv_c, g_lru_c),
                                   w_branch[l], w_gate[l], b_gate[l], w_out[l])
            ctx = deepnorm_post(ctx, gate_c * out_c, ln_g[l], ln_b[l])
        x = x_new
    return x
```

```python
import functools

import jax
import jax.numpy as jnp
import numpy as np
from jax import lax
from jax.experimental import pallas as pl
from jax.experimental.pallas import tpu as pltpu

F32 = jnp.float32
BF16 = jnp.bfloat16

D_MODEL = 1024
DEPTH = 4
GRID_W = 64
BRANCH_W = D_MODEL // 2
N_BRANCH = 3
HEAD_DIM = 64
N_HEADS = BRANCH_W // HEAD_DIM
N_KV_HEADS = N_HEADS // 4
GROUP = N_HEADS // N_KV_HEADS
KV_W = N_KV_HEADS * HEAD_DIM
WINDOW = 128
BLOCK = 128
ROPE_BASE = 10000.0
CONV_K = 31
CONV_HALO = 16
LRU_BLOCKS = 8
LRU_CONV_K = 4
LRU_C = 8.0
DEEPNORM_ALPHA = (2 * DEPTH) ** 0.25
LN_EPS = 1e-6
ATTN_SCALE = HEAD_DIM ** -0.5

V7X_LANES = 128
V7X_SUBLANES = 8
VMEM_LIMIT_BYTES = 56 * 1024 * 1024


def _layer_norm(x):
    mu = jnp.mean(x, axis=-1, keepdims=True)
    xc = x - mu
    var = jnp.mean(xc * xc, axis=-1, keepdims=True)
    return xc * lax.rsqrt(var + LN_EPS)


def _sigmoid(x):
    return jax.nn.sigmoid(x)


def _silu(x):
    return x * jax.nn.sigmoid(x)


def _dot(a, b):
    return jnp.dot(a, b, preferred_element_type=F32)


def _dot_nt(a, b):
    return lax.dot_general(a, b, (((1,), (1,)), ((), ())), preferred_element_type=F32)


def _ln_mod(x, mod):
    shift = mod[:, :D_MODEL]
    scale = mod[:, D_MODEL:2 * D_MODEL]
    return (_layer_norm(x) * (1.0 + scale) + shift).astype(BF16)


def _mods_kernel(c_ref, w_ref, b_ref, o_ref):
    s = _silu(c_ref[...])
    o_ref[0] = _dot(s.astype(BF16), w_ref[0].astype(BF16)) + b_ref[0]


def _mods_call(c_rows, w_ada, b_ada):
    depth = w_ada.shape[0]
    ncol = 3
    return pl.pallas_call(
        _mods_kernel,
        grid=(depth, ncol),
        in_specs=[
            pl.BlockSpec((V7X_SUBLANES, D_MODEL), lambda l, j: (0, 0)),
            pl.BlockSpec((1, D_MODEL, D_MODEL), lambda l, j: (l, 0, j)),
            pl.BlockSpec((1, 1, D_MODEL), lambda l, j: (l, 0, j)),
        ],
        out_specs=pl.BlockSpec((1, V7X_SUBLANES, D_MODEL), lambda l, j: (l, 0, j)),
        out_shape=jax.ShapeDtypeStruct((depth, V7X_SUBLANES, 3 * D_MODEL), F32),
        compiler_params=pltpu.CompilerParams(
            dimension_semantics=("arbitrary", "arbitrary"), vmem_limit_bytes=VMEM_LIMIT_BYTES),
        name="adaln_mods",
    )(c_rows, w_ada, b_ada.reshape(depth, 1, 3 * D_MODEL))


def _rglru_gates(xc, w_ax_ref, b_ax_ref, lam_ref):
    ax = _dot(xc.astype(BF16), w_ax_ref[...]) + b_ax_ref[...]
    r = _sigmoid(ax[:, :BRANCH_W])
    ig = _sigmoid(ax[:, BRANCH_W:])
    z = -lam_ref[...]
    softplus = jnp.maximum(z, 0.0) + jnp.log1p(jnp.exp(-jnp.abs(z)))
    log_a = (-LRU_C) * r * softplus
    a = jnp.exp(log_a)
    u = jnp.sqrt(jnp.tanh(-log_a) * (a * a + 1.0)) * ig * xc
    return a, u


def _scan_rows(a, u, reverse):
    n = a.shape[0]
    row = lax.broadcasted_iota(jnp.int32, a.shape, 0)
    d = 1
    while d < n:
        if reverse:
            a_sh = pltpu.roll(a, n - d, axis=0)
            u_sh = pltpu.roll(u, n - d, axis=0)
            m = row < n - d
        else:
            a_sh = pltpu.roll(a, d, axis=0)
            u_sh = pltpu.roll(u, d, axis=0)
            m = row >= d
        u = jnp.where(m, a * u_sh + u, u)
        a = jnp.where(m, a * a_sh, a)
        d *= 2
    return a, u


def _lru_bwd_kernel(*refs, T, S, halo):
    if halo:
        (x_ref, xh_ref, mod_ref, wxl_ref, cw_ref, cb_ref, wax_ref, bax_ref, lam_ref, seed_ref,
         hb_ref, st_ref, buf_ref, carry_ref) = refs
    else:
        (x_ref, mod_ref, wxl_ref, cw_ref, cb_ref, wax_ref, bax_ref, lam_ref, seed_ref,
         hb_ref, st_ref, buf_ref, carry_ref) = refs
    i = pl.program_id(1)
    mod = mod_ref[0]

    @pl.when(i == 0)
    def _():
        carry_ref[...] = seed_ref[0]

    xl = _dot(_ln_mod(x_ref[0], mod), wxl_ref[...])
    buf_ref[0:T, :] = xl
    if halo:
        xlh = _dot(_ln_mod(xh_ref[0], mod), wxl_ref[...])
        buf_ref[T:T + V7X_SUBLANES, :] = jnp.where(i == 0, 0.0, xlh)
    else:
        buf_ref[T:T + V7X_SUBLANES, :] = jnp.zeros((V7X_SUBLANES, BRANCH_W), F32)
    xc = cb_ref[...] + cw_ref[0:1, :] * xl
    for k in range(1, LRU_CONV_K):
        xc = xc + cw_ref[k:k + 1, :] * buf_ref[k:k + T, :]
    a, u = _rglru_gates(xc, wax_ref, bax_ref, lam_ref)
    a_cum, h0 = _scan_rows(a, u, reverse=True)
    h = h0 + a_cum * carry_ref[...]
    hb_ref[0] = h
    carry_ref[...] = h[0:1, :]
    st_ref[0] = h[0:1, :]


def _lru_bwd_call(x, mod, w_xl, conv_w, conv_b, w_ax, b_ax, lam, seed, *, T):
    B, S, _ = x.shape
    nT = S // T
    halo = nT > 1
    W = BRANCH_W
    rev = lambda b, i: (b, nT - 1 - i, 0)
    const2 = lambda b, i: (0, 0)
    in_specs = [pl.BlockSpec((1, T, D_MODEL), rev)]
    args = [x]
    if halo:
        hb = T // V7X_SUBLANES
        last = S // V7X_SUBLANES - 1
        in_specs.append(pl.BlockSpec((1, V7X_SUBLANES, D_MODEL),
                                     lambda b, i: (b, jnp.minimum((nT - i) * hb, last), 0)))
        args.append(x)
    in_specs += [
        pl.BlockSpec((1, 1, 3 * D_MODEL), lambda b, i: (b, 0, 0)),
        pl.BlockSpec((D_MODEL, W), const2),
        pl.BlockSpec((LRU_CONV_K, W), const2),
        pl.BlockSpec((1, W), const2),
        pl.BlockSpec((W, 2 * W), const2),
        pl.BlockSpec((1, 2 * W), const2),
        pl.BlockSpec((1, W), const2),
        pl.BlockSpec((1, 1, W), lambda b, i: (b, 0, 0)),
    ]
    args += [mod, w_xl, conv_w, conv_b, w_ax, b_ax, lam, seed]
    return pl.pallas_call(
        functools.partial(_lru_bwd_kernel, T=T, S=S, halo=halo),
        grid=(B, nT),
        in_specs=in_specs,
        out_specs=[pl.BlockSpec((1, T, W), rev), pl.BlockSpec((1, 1, W), lambda b, i: (b, 0, 0))],
        out_shape=[jax.ShapeDtypeStruct((B, S, W), F32), jax.ShapeDtypeStruct((B, 1, W), F32)],
        scratch_shapes=[pltpu.VMEM((T + V7X_SUBLANES, W), F32), pltpu.VMEM((1, W), F32)],
        compiler_params=pltpu.CompilerParams(
            dimension_semantics=("arbitrary", "arbitrary"), vmem_limit_bytes=VMEM_LIMIT_BYTES),
        name="lru_bwd",
    )(*args)


def _rope(x, cos, sin_signed, even):
    outs = []
    for j in range(x.shape[1] // V7X_LANES):
        xs = x[:, j * V7X_LANES:(j + 1) * V7X_LANES]
        up = pltpu.roll(xs, V7X_LANES - HEAD_DIM // 4, axis=1)
        dn = pltpu.roll(xs, HEAD_DIM // 4, axis=1)
        outs.append(xs * cos + jnp.where(even, up, dn) * sin_signed)
    return outs[0] if len(outs) == 1 else jnp.concatenate(outs, axis=1)


def _mix_kernel(*refs, T, S, local):
    W = BRANCH_W
    if local:
        (x_ref, xl_ref, xr_ref, cos_ref, cosl_ref, cosr_ref, sin_ref, sinl_ref, sinr_ref, bias_ref,
         kc_ref, vc_ref, hb_ref, mod_ref, wq_ref, wg_ref, wkv_ref, wcl_ref, sink_ref,
         dww_ref, dwb_ref, clg_ref, clb_ref,
         lcw_ref, lcb_ref, wax_ref, bax_ref, lam_ref, seed_ref,
         y_ref, st_ref,
         hext_ref, kext_ref, vext_ref, yatt_ref, cbuf_ref, lbuf_ref, carry_ref) = refs
        halo = BLOCK
    else:
        (x_ref, hb_ref, mod_ref, wq_ref, wg_ref, wkv_ref, wcl_ref, sink_ref,
         dww_ref, dwb_ref, clg_ref, clb_ref,
         lcw_ref, lcb_ref, wax_ref, bax_ref, lam_ref, seed_ref,
         y_ref, st_ref, ko_ref, vo_ref,
         hext_ref, kext_ref, vext_ref, yatt_ref, cbuf_ref, lbuf_ref, carry_ref) = refs
        halo = 0
    i = pl.program_id(1)
    nT = S // T
    mod = mod_ref[0]

    @pl.when(i == 0)
    def _():
        carry_ref[...] = seed_ref[0]

    hext_ref[halo:halo + T, :] = _ln_mod(x_ref[0], mod)
    if local:
        hext_ref[0:halo, :] = _ln_mod(xl_ref[0], mod)
        hext_ref[halo + T:halo + T + halo, :] = _ln_mod(xr_ref[0], mod)
    hc = hext_ref[halo:halo + T, :]

    if local:
        lane = lax.broadcasted_iota(jnp.int32, (1, V7X_LANES), 1)
        even = (lane & (HEAD_DIM // 4)) == 0
        parts = ((0, halo, cosl_ref, sinl_ref), (halo, T, cos_ref, sin_ref), (halo + T, halo, cosr_ref, sinr_ref))
        for r0, n, c_ref, s_ref in parts:
            kv = _dot(hext_ref[r0:r0 + n, :], wkv_ref[...])
            kext_ref[r0:r0 + n, :] = _rope(kv[:, :KV_W], c_ref[...], s_ref[...], even).astype(BF16)
            vext_ref[r0:r0 + n, :] = kv[:, KV_W:].astype(BF16)
        q = _rope(_dot(hc, wq_ref[...]), cos_ref[...], sin_ref[...], even) * ATTN_SCALE
    else:
        kv = _dot(hc, wkv_ref[...])
        kext_ref[...] = kv[:, :KV_W].astype(BF16)
        vext_ref[...] = kv[:, KV_W:].astype(BF16)
        ko_ref[0] = kext_ref[...]
        vo_ref[0] = vext_ref[...]
        q = _dot(hc, wq_ref[...]) * ATTN_SCALE

    for jb in range(T // BLOCK):
        r0 = jb * BLOCK
        if local:
            pos0 = i * T + r0
            bidx = jnp.where(pos0 == 0, 1, jnp.where(pos0 == S - BLOCK, 2, 0))
            bias = bias_ref[bidx]
        for hk in range(N_KV_HEADS):
            heads = [GROUP * hk + g for g in range(GROUP)]
            q4 = jnp.concatenate(
                [q[r0:r0 + BLOCK, h * HEAD_DIM:(h + 1) * HEAD_DIM] for h in heads], axis=0).astype(BF16)
            sink = jnp.concatenate(
                [jnp.broadcast_to(sink_ref[h:h + 1, 0:1], (BLOCK, 1)) for h in heads], axis=0)
            cs = slice(hk * HEAD_DIM, (hk + 1) * HEAD_DIM)
            if local:
                k_c, v_c = kc_ref[0, :, cs], vc_ref[0, :, cs]
            else:
                k_c, v_c = kext_ref[:, cs], vext_ref[:, cs]
            s_ctx = _dot_nt(q4, k_c)
            m = jnp.maximum(jnp.max(s_ctx, axis=-1, keepdims=True), sink)
            if local:
                k_w = kext_ref[r0:r0 + 3 * BLOCK, cs]
                v_w = vext_ref[r0:r0 + 3 * BLOCK, cs]
                s_loc = _dot_nt(q4, k_w)
                s_loc = jnp.concatenate(
                    [s_loc[g * BLOCK:(g + 1) * BLOCK] + bias for g in range(GROUP)], axis=0)
                m = jnp.maximum(m, jnp.max(s_loc, axis=-1, keepdims=True))
                p_loc = jnp.exp(s_loc - m)
            p_ctx = jnp.exp(s_ctx - m)
            den = jnp.sum(p_ctx, axis=-1, keepdims=True) + jnp.exp(sink - m)
            o = _dot(p_ctx.astype(BF16), v_c)
            if local:
                den = den + jnp.sum(p_loc, axis=-1, keepdims=True)
                o = o + _dot(p_loc.astype(BF16), v_w)
            o = o / den
            for g, h in enumerate(heads):
                yatt_ref[r0:r0 + BLOCK, h * HEAD_DIM:(h + 1) * HEAD_DIM] = o[g * BLOCK:(g + 1) * BLOCK]

    g_att = _dot(hc, wg_ref[:, 0:W])
    y_ref[0, :, 0:W] = (yatt_ref[...] * _silu(g_att)).astype(BF16)

    n_ext = T + 2 * CONV_HALO
    if local:
        ucl = _dot(hext_ref[halo - CONV_HALO:halo + T + CONV_HALO, :], wcl_ref[...])
        pos = i * T - CONV_HALO + lax.broadcasted_iota(jnp.int32, (n_ext, 1), 0)
        inside = (pos >= 0) & (pos < S)
        glu = jnp.where(inside, ucl[:, 0:W] * _sigmoid(ucl[:, W:2 * W]), 0.0)
        cbuf_ref[...] = glu
        lbuf_ref[...] = jnp.where(inside, ucl[:, 2 * W:3 * W], 0.0)
    else:
        ucl = _dot(hc, wcl_ref[...])
        zeros = jnp.zeros((CONV_HALO, W), F32)
        cbuf_ref[0:CONV_HALO, :] = zeros
        cbuf_ref[CONV_HALO + T:n_ext, :] = zeros
        lbuf_ref[0:CONV_HALO, :] = zeros
        lbuf_ref[CONV_HALO + T:n_ext, :] = zeros
        cbuf_ref[CONV_HALO:CONV_HALO + T, :] = ucl[:, 0:W] * _sigmoid(ucl[:, W:2 * W])
        lbuf_ref[CONV_HALO:CONV_HALO + T, :] = ucl[:, 2 * W:3 * W]

    acc = jnp.broadcast_to(dwb_ref[...], (T, W))
    off = CONV_HALO - CONV_K // 2
    for k in range(CONV_K):
        acc = acc + dww_ref[k:k + 1, :] * cbuf_ref[off + k:off + k + T, :]
    y_conv = _silu(_layer_norm(acc) * clg_ref[...] + clb_ref[...])
    g_conv = _dot(hc, wg_ref[:, W:2 * W])
    y_ref[0, :, W:2 * W] = (y_conv * _silu(g_conv)).astype(BF16)

    xc = jnp.broadcast_to(lcb_ref[...], (T, W))
    off = CONV_HALO - (LRU_CONV_K - 1)
    for k in range(LRU_CONV_K):
        xc = xc + lcw_ref[k:k + 1, :] * lbuf_ref[off + k:off + k + T, :]
    a, u = _rglru_gates(xc, wax_ref, bax_ref, lam_ref)
    a_cum, h0 = _scan_rows(a, u, reverse=False)
    hf = h0 + a_cum * carry_ref[...]
    carry_ref[...] = hf[T - 1:T, :]
    st_ref[0] = hf[T - 1:T, :]
    g_lru = _dot(hc, wg_ref[:, 2 * W:3 * W])
    y_ref[0, :, 2 * W:3 * W] = ((hf + hb_ref[0]) * _silu(g_lru)).astype(BF16)


def _mix_call(x, hb, mod, wts, seed, *, T, local, rope=None, bias=None, kc=None, vc=None):
    B, S, _ = x.shape
    nT = S // T
    W = BRANCH_W
    tile = lambda b, i: (b, i, 0)
    batch = lambda b, i: (b, 0, 0)
    const2 = lambda b, i: (0, 0)
    in_specs, args = [pl.BlockSpec((1, T, D_MODEL), tile)], [x]
    if local:
        nb = T // BLOCK
        last = S // BLOCK - 1
        left = lambda b, i: (b, jnp.maximum(i * nb - 1, 0), 0)
        right = lambda b, i: (b, jnp.minimum((i + 1) * nb, last), 0)
        in_specs += [pl.BlockSpec((1, BLOCK, D_MODEL), left), pl.BlockSpec((1, BLOCK, D_MODEL), right)]
        args += [x, x]
        cos, sin = rope
        for tab in (cos, sin):
            in_specs += [
                pl.BlockSpec((T, V7X_LANES), lambda b, i: (i, 0)),
                pl.BlockSpec((BLOCK, V7X_LANES), lambda b, i: (jnp.maximum(i * nb - 1, 0), 0)),
                pl.BlockSpec((BLOCK, V7X_LANES), lambda b, i: (jnp.minimum((i + 1) * nb, last), 0)),
            ]
            args += [tab, tab, tab]
        C = kc.shape[1]
        in_specs += [
            pl.BlockSpec((3, BLOCK, 3 * BLOCK), lambda b, i: (0, 0, 0)),
            pl.BlockSpec((1, C, KV_W), batch),
            pl.BlockSpec((1, C, KV_W), batch),
        ]
        args += [bias, kc, vc]
    in_specs += [
        pl.BlockSpec((1, T, W), tile),
        pl.BlockSpec((1, 1, 3 * D_MODEL), batch),
        pl.BlockSpec((D_MODEL, W), const2),
        pl.BlockSpec((D_MODEL, 3 * W), const2),
        pl.BlockSpec((D_MODEL, 2 * KV_W), const2),
        pl.BlockSpec((D_MODEL, 3 * W), const2),
        pl.BlockSpec((N_HEADS, V7X_LANES), const2),
        pl.BlockSpec((CONV_K, W), const2),
        pl.BlockSpec((1, W), const2),
        pl.BlockSpec((1, W), const2),
        pl.BlockSpec((1, W), const2),
        pl.BlockSpec((LRU_CONV_K, W), const2),
        pl.BlockSpec((1, W), const2),
        pl.BlockSpec((W, 2 * W), const2),
        pl.BlockSpec((1, 2 * W), const2),
        pl.BlockSpec((1, W), const2),
        pl.BlockSpec((1, 1, W), batch),
    ]
    args += [hb, mod, wts["w_q"], wts["w_g"], wts["w_kv"], wts["w_cl"], wts["sink"],
             wts["dw_w"], wts["dw_b"], wts["cl_g"], wts["cl_b"],
             wts["lcw_f"], wts["lcb_f"], wts["wax_f"], wts["bax_f"], wts["lam_f"], seed]
    out_specs = [pl.BlockSpec((1, T, 3 * W), tile), pl.BlockSpec((1, 1, W), batch)]
    out_shape = [jax.ShapeDtypeStruct((B, S, 3 * W), BF16), jax.ShapeDtypeStruct((B, 1, W), F32)]
    if not local:
        out_specs += [pl.BlockSpec((1, T, KV_W), tile), pl.BlockSpec((1, T, KV_W), tile)]
        out_shape += [jax.ShapeDtypeStruct((B, S, KV_W), BF16), jax.ShapeDtypeStruct((B, S, KV_W), BF16)]
    halo = BLOCK if local else 0
    n_ext = T + 2 * CONV_HALO
    scratch = [
        pltpu.VMEM((T + 2 * halo, D_MODEL), BF16),
        pltpu.VMEM((T + 2 * halo, KV_W), BF16),
        pltpu.VMEM((T + 2 * halo, KV_W), BF16),
        pltpu.VMEM((T, W), F32),
        pltpu.VMEM((n_ext, W), F32),
        pltpu.VMEM((n_ext, W), F32),
        pltpu.VMEM((1, W), F32),
    ]
    return pl.pallas_call(
        functools.partial(_mix_kernel, T=T, S=S, local=local),
        grid=(B, nT),
        in_specs=in_specs,
        out_specs=out_specs,
        out_shape=out_shape,
        scratch_shapes=scratch,
        compiler_params=pltpu.CompilerParams(
            dimension_semantics=("arbitrary", "arbitrary"), vmem_limit_bytes=VMEM_LIMIT_BYTES),
        name="mix_local" if local else "mix_ctx",
    )(*args)


def _merge_kernel(x_ref, y_ref, mod_ref, wgate_ref, bgate_ref, wbr_ref, wout_ref, lng_ref, lnb_ref, o_ref):
    x = x_ref[0]
    mod = mod_ref[0]
    h = _ln_mod(x, mod)
    m = None
    for n in range(N_BRANCH):
        cols = slice(n * D_MODEL, (n + 1) * D_MODEL)
        g = _sigmoid(_dot(h, wgate_ref[:, cols]) + bgate_ref[:, cols])
        p = _dot(y_ref[0, :, n * BRANCH_W:(n + 1) * BRANCH_W], wbr_ref[n])
        m = p * g if m is None else m + p * g
    out = _dot(m.astype(BF16), wout_ref[...])
    gate = mod[:, 2 * D_MODEL:3 * D_MODEL]
    z = DEEPNORM_ALPHA * x + gate * out
    o_ref[0] = _layer_norm(z) * lng_ref[...] + lnb_ref[...]


def _merge_call(x, y, mod, wts, *, T):
    B, S, _ = x.shape
    nT = S // T
    tile = lambda b, i: (b, i, 0)
    const2 = lambda b, i: (0, 0)
    return pl.pallas_call(
        _merge_kernel,
        grid=(B, nT),
        in_specs=[
            pl.BlockSpec((1, T, D_MODEL), tile),
            pl.BlockSpec((1, T, 3 * BRANCH_W), tile),
            pl.BlockSpec((1, 1, 3 * D_MODEL), lambda b, i: (b, 0, 0)),
            pl.BlockSpec((D_MODEL, 3 * D_MODEL), const2),
            pl.BlockSpec((1, 3 * D_MODEL), const2),
            pl.BlockSpec((N_BRANCH, BRANCH_W, D_MODEL), lambda b, i: (0, 0, 0)),
            pl.BlockSpec((D_MODEL, D_MODEL), const2),
            pl.BlockSpec((1, D_MODEL), const2),
            pl.BlockSpec((1, D_MODEL), const2),
        ],
        out_specs=pl.BlockSpec((1, T, D_MODEL), tile),
        out_shape=jax.ShapeDtypeStruct((B, S, D_MODEL), F32),
        compiler_params=pltpu.CompilerParams(
            dimension_semantics=("arbitrary", "arbitrary"), vmem_limit_bytes=VMEM_LIMIT_BYTES),
        name="merge",
    )(x, y, mod, wts["w_gate"], wts["b_gate"], wts["w_branch"], wts["w_out"], wts["ln_g"], wts["ln_b"])


def _rope_tables(S):
    quarter = HEAD_DIM // 4
    inv = ROPE_BASE ** (-jnp.arange(quarter, dtype=F32) / quarter)
    pos = jnp.arange(S)
    ang_r = (pos // GRID_W).astype(F32)[:, None] * inv
    ang_c = (pos % GRID_W).astype(F32)[:, None] * inv
    ang = jnp.concatenate([ang_r, ang_r, ang_c, ang_c] * 2, axis=-1)
    sign = jnp.tile(jnp.concatenate([-jnp.ones(quarter, F32), jnp.ones(quarter, F32)]), 2 * V7X_LANES // HEAD_DIM)
    return jnp.cos(ang), jnp.sin(ang) * sign


def _window_bias():
    r = np.arange(BLOCK)[:, None]
    j = np.arange(3 * BLOCK)[None, :]
    band = np.abs(j - BLOCK - r) <= WINDOW
    first = band & (j >= BLOCK)
    last = band & (j < 2 * BLOCK)
    return jnp.asarray(np.where(np.stack([band, first, last]), 0.0, -np.inf), dtype=F32)


def _block_diag(w):
    n, d, e = w.shape
    eye = jnp.eye(n, dtype=w.dtype)
    return (eye[:, None, :, None] * w[:, :, None, :]).reshape(n * d, n * e)


def _layer_weights(l, w_in, attn_sink, conv_dw_w, conv_dw_b, conv_ln_g, conv_ln_b, lru_conv_w, lru_conv_b,
                   lru_w_a, lru_b_a, lru_w_x, lru_b_x, lru_lambda, w_branch, w_gate, b_gate, w_out, ln_g, ln_b):
    W = BRANCH_W
    wi = w_in[l].astype(BF16)
    c = np.cumsum((0, N_HEADS * HEAD_DIM, KV_W, KV_W, W, 2 * W, W, W, W)).tolist()
    w_q, w_k, w_v, w_ga, w_uc, w_gc, w_xl, w_gl = (wi[:, c[n]:c[n + 1]] for n in range(8))
    row = lambda v: v.reshape(1, -1)
    wts = {
        "w_q": w_q,
        "w_g": jnp.concatenate([w_ga, w_gc, w_gl], axis=1),
        "w_kv": jnp.concatenate([w_k, w_v], axis=1),
        "w_cl": jnp.concatenate([w_uc, w_xl], axis=1),
        "w_xl": w_xl,
        "sink": jnp.broadcast_to(attn_sink[l][:, None], (N_HEADS, V7X_LANES)),
        "dw_w": conv_dw_w[l], "dw_b": row(conv_dw_b[l]), "cl_g": row(conv_ln_g[l]), "cl_b": row(conv_ln_b[l]),
        "w_gate": w_gate[l].astype(BF16), "b_gate": row(b_gate[l]),
        "w_branch": w_branch[l].astype(BF16), "w_out": w_out[l].astype(BF16),
        "ln_g": row(ln_g[l]), "ln_b": row(ln_b[l]),
    }
    for d, tag in ((0, "f"), (1, "b")):
        wts["lcw_" + tag] = lru_conv_w[l, d]
        wts["lcb_" + tag] = row(lru_conv_b[l, d])
        wts["wax_" + tag] = jnp.concatenate(
            [_block_diag(lru_w_a[l, d]), _block_diag(lru_w_x[l, d])], axis=1).astype(BF16)
        wts["bax_" + tag] = row(jnp.concatenate([lru_b_a[l, d], lru_b_x[l, d]]))
        wts["lam_" + tag] = row(lru_lambda[l, d])
    return wts


TILE_T = 512


def kernel(x, c, ctx, c_ctx, w_ada, b_ada, w_in, attn_sink, conv_dw_w, conv_dw_b, conv_ln_g, conv_ln_b,
           lru_conv_w, lru_conv_b, lru_w_a, lru_b_a, lru_w_x, lru_b_x, lru_lambda,
           w_branch, w_gate, b_gate, w_out, ln_g, ln_b):
    B, S, D = x.shape
    C = ctx.shape[1]
    depth = w_ada.shape[0]
    assert D == D_MODEL and S % TILE_T == 0 and S >= 2 * BLOCK and C % V7X_SUBLANES == 0

    c_rows = jnp.concatenate([c, c_ctx[None, :], jnp.zeros((V7X_SUBLANES - B - 1, D), F32)], axis=0)
    mods = _mods_call(c_rows, w_ada, b_ada)
    rope = _rope_tables(S)
    bias = _window_bias()
    zero_seed = jnp.zeros((B, 1, BRANCH_W), F32)

    for l in range(depth):
        wts = _layer_weights(l, w_in, attn_sink, conv_dw_w, conv_dw_b, conv_ln_g, conv_ln_b, lru_conv_w,
                             lru_conv_b, lru_w_a, lru_b_a, lru_w_x, lru_b_x, lru_lambda,
                             w_branch, w_gate, b_gate, w_out, ln_g, ln_b)
        mod_lat = mods[l, 0:B][:, None, :]
        mod_ctx = jnp.broadcast_to(mods[l, B][None, None, :], (B, 1, 3 * D))
        lru_b = (wts["w_xl"], wts["lcw_b"], wts["lcb_b"], wts["wax_b"], wts["bax_b"], wts["lam_b"])

        hb_c, seed_b = _lru_bwd_call(ctx, mod_ctx, *lru_b, zero_seed, T=C)
        y_c, seed_f, kc, vc = _mix_call(ctx, hb_c, mod_ctx, wts, zero_seed, T=C, local=False)

        hb = _lru_bwd_call(x, mod_lat, *lru_b, seed_b, T=TILE_T)[0]
        y = _mix_call(x, hb, mod_lat, wts, seed_f, T=TILE_T, local=True, rope=rope, bias=bias, kc=kc, vc=vc)[0]
        x = _merge_call(x, y, mod_lat, wts, T=TILE_T)
        if l != depth - 1:
            ctx = _merge_call(ctx, y_c, mod_ctx, wts, T=C)
    return x
```

```python
import functools

import jax
import jax.numpy as jnp
import numpy as np
from jax import lax
from jax.experimental import pallas as pl
from jax.experimental.pallas import tpu as pltpu

F32 = jnp.float32
BF16 = jnp.bfloat16

D_MODEL = 1024
DEPTH = 4
GRID_W = 64
BRANCH_W = D_MODEL // 2
N_BRANCH = 3
HEAD_DIM = 64
N_HEADS = BRANCH_W // HEAD_DIM
N_KV_HEADS = N_HEADS // 4
GROUP = N_HEADS // N_KV_HEADS
KV_W = N_KV_HEADS * HEAD_DIM
WINDOW = 128
BLOCK = 128
ROPE_BASE = 10000.0
CONV_K = 31
CONV_HALO = 16
LRU_BLOCKS = 8
LRU_CONV_K = 4
LRU_C = 8.0
DEEPNORM_ALPHA = (2 * DEPTH) ** 0.25
LN_EPS = 1e-6
ATTN_SCALE = HEAD_DIM ** -0.5

_IN_SIZES = (N_HEADS * HEAD_DIM, KV_W, KV_W, BRANCH_W, 2 * BRANCH_W, BRANCH_W, BRANCH_W, BRANCH_W)
_IN_OFF = np.cumsum((0,) + _IN_SIZES).tolist()
IN_COLS = _IN_OFF[-1]
COL_Q, COL_K, COL_V, COL_GATT, COL_UCONV, COL_GCONV, COL_XLRU, COL_GLRU = _IN_OFF[:-1]

V7X_LANES = 128
V7X_SUBLANES = 8
VMEM_LIMIT_BYTES = 56 * 1024 * 1024
N_SLAB = BRANCH_W // V7X_LANES


def _seg_len(T):
    L = (T + 2 * CONV_HALO) // V7X_SUBLANES
    assert V7X_SUBLANES * L == T + 2 * CONV_HALO and L % 8 == 4, (T, L)
    return L


def _layer_norm(x):
    mu = jnp.mean(x, axis=-1, keepdims=True)
    xc = x - mu
    var = jnp.mean(xc * xc, axis=-1, keepdims=True)
    return xc * lax.rsqrt(var + LN_EPS)


def _sigmoid(x):
    return jax.nn.sigmoid(x)


def _silu(x):
    return x * jax.nn.sigmoid(x)


def _dot(a, b):
    return jnp.dot(a, b, preferred_element_type=F32)


def _dot_nt(a, b):
    return lax.dot_general(a, b, (((1,), (1,)), ((), ())), preferred_element_type=F32)


def _ln_mod(x, mod):
    shift = mod[:, :D_MODEL]
    scale = mod[:, D_MODEL:2 * D_MODEL]
    return (_layer_norm(x) * (1.0 + scale) + shift).astype(BF16)


def _mod_row(mods_ref, row):
    if isinstance(row, int):
        return mods_ref[0, row:row + 1, :]
    return mods_ref[0, pl.ds(row, 1), :]


def _lane_slabs(x):
    return [x[:, c * V7X_LANES:(c + 1) * V7X_LANES] for c in range(x.shape[1] // V7X_LANES)]


def _mods_kernel(c_ref, w_ref, b_ref, o_ref):
    s = _silu(c_ref[...])
    o_ref[0] = _dot(s.astype(BF16), w_ref[0].astype(BF16)) + b_ref[0]


def _mods_call(c_rows, w_ada, b_ada):
    depth = w_ada.shape[0]
    ncol = 3
    return pl.pallas_call(
        _mods_kernel,
        grid=(depth, ncol),
        in_specs=[
            pl.BlockSpec((V7X_SUBLANES, D_MODEL), lambda l, j: (0, 0)),
            pl.BlockSpec((1, D_MODEL, D_MODEL), lambda l, j: (l, 0, j)),
            pl.BlockSpec((1, 1, D_MODEL), lambda l, j: (l, 0, j)),
        ],
        out_specs=pl.BlockSpec((1, V7X_SUBLANES, D_MODEL), lambda l, j: (l, 0, j)),
        out_shape=jax.ShapeDtypeStruct((depth, V7X_SUBLANES, 3 * D_MODEL), F32),
        compiler_params=pltpu.CompilerParams(
            dimension_semantics=("arbitrary", "arbitrary"), vmem_limit_bytes=VMEM_LIMIT_BYTES),
        name="adaln_mods",
    )(c_rows, w_ada, b_ada.reshape(depth, 1, 3 * D_MODEL))


def _rglru_gates(xc, wax_ref, bax_ref, lam_ref):
    ax = _dot(xc.astype(BF16), wax_ref[0]) + bax_ref[0]
    r = _sigmoid(ax[:, :BRANCH_W])
    ig = _sigmoid(ax[:, BRANCH_W:])
    z = -lam_ref[0]
    softplus = jnp.maximum(z, 0.0) + jnp.log1p(jnp.exp(-jnp.abs(z)))
    log_a = (-LRU_C) * r * softplus
    a = jnp.exp(log_a)
    u = jnp.sqrt(jnp.tanh(-log_a) * (a * a + 1.0)) * ig * xc
    return a, u


def _scan_segments(a, u, carry, abuf, ubuf, hbuf, *, T, reverse):
    L = _seg_len(T)
    rows = V7X_SUBLANES * L
    for c, (a_c, u_c) in enumerate(zip(_lane_slabs(a), _lane_slabs(u))):
        abuf[c, 0:T, :] = a_c
        ubuf[c, 0:T, :] = u_c
        abuf[c, T:rows, :] = jnp.ones((rows - T, V7X_LANES), F32)
        ubuf[c, T:rows, :] = jnp.zeros((rows - T, V7X_LANES), F32)
    order = range(L - 1, -1, -1) if reverse else range(L)
    sub = lax.broadcasted_iota(jnp.int32, (V7X_SUBLANES, V7X_LANES), 0)
    seg = lambda buf, c, j: buf[c, pl.ds(j, V7X_SUBLANES, stride=L), :]
    finals = []
    for c, cin in enumerate(_lane_slabs(carry)):
        h = jnp.zeros((V7X_SUBLANES, V7X_LANES), F32)
        p = jnp.ones((V7X_SUBLANES, V7X_LANES), F32)
        for j in order:
            a_j = seg(abuf, c, j)
            h = a_j * h + seg(ubuf, c, j)
            p = a_j * p
        for d in (1, 2, 4):
            sh = V7X_SUBLANES - d if reverse else d
            m = (sub < V7X_SUBLANES - d) if reverse else (sub >= d)
            h = jnp.where(m, p * pltpu.roll(h, sh, axis=0) + h, h)
            p = jnp.where(m, p * pltpu.roll(p, sh, axis=0), p)
        end = h + p * cin
        if reverse:
            h = jnp.where(sub == V7X_SUBLANES - 1, cin, pltpu.roll(end, V7X_SUBLANES - 1, axis=0))
            finals.append(end[0:1, :])
        else:
            h = jnp.where(sub == 0, cin, pltpu.roll(end, 1, axis=0))
            finals.append(end[V7X_SUBLANES - 1:V7X_SUBLANES, :])
        for j in order:
            h = seg(abuf, c, j) * h + seg(ubuf, c, j)
            hbuf[c, pl.ds(j, V7X_SUBLANES, stride=L), :] = h
    h_all = jnp.concatenate([hbuf[c, 0:T, :] for c in range(N_SLAB)], axis=1)
    return h_all, jnp.concatenate(finals, axis=1)


def _dwconv_segments(ybuf, yext, obuf, w_ref, b_ref, *, T, K):
    L = _seg_len(T)
    half = K // 2
    for c in range(N_SLAB):
        for j in range(L):
            yext[c, half + j] = ybuf[c, pl.ds(j, V7X_SUBLANES, stride=L), :]
        for m in range(1, half + 1):
            yext[c, half - m] = pltpu.roll(yext[c, half + L - m], 1, axis=0)
        for m in range(half):
            yext[c, half + L + m] = pltpu.roll(yext[c, half + m], V7X_SUBLANES - 1, axis=0)
        lanes = slice(c * V7X_LANES, (c + 1) * V7X_LANES)
        wk = [jnp.broadcast_to(w_ref[0, k:k + 1, lanes], (V7X_SUBLANES, V7X_LANES)) for k in range(K)]
        bc = jnp.broadcast_to(b_ref[0, :, lanes], (V7X_SUBLANES, V7X_LANES))
        for j in range(L):
            acc = bc
            for k in range(K):
                acc = acc + wk[k] * yext[c, j + k]
            obuf[c, pl.ds(j, V7X_SUBLANES, stride=L), :] = acc
    return jnp.concatenate([obuf[c, 0:T, :] for c in range(N_SLAB)], axis=1)


def _lru_scratch(T):
    rows = V7X_SUBLANES * _seg_len(T)
    return [pltpu.VMEM((N_SLAB, rows, V7X_LANES), F32) for _ in range(3)]


def _lru_bwd_kernel(*refs, T, halo, mod_row):
    if halo:
        (x_ref, xh_ref, mods_ref, wlo_ref, whi_ref, cw_ref, cb_ref, wax_ref, bax_ref, lam_ref, seed_ref,
         hb_ref, st_ref, buf_ref, carry_ref, abuf, ubuf, hbuf) = refs
    else:
        (x_ref, mods_ref, wlo_ref, whi_ref, cw_ref, cb_ref, wax_ref, bax_ref, lam_ref, seed_ref,
         hb_ref, st_ref, buf_ref, carry_ref, abuf, ubuf, hbuf) = refs
    i = pl.program_id(1)
    mod = _mod_row(mods_ref, pl.program_id(0) if mod_row is None else mod_row)

    @pl.when(i == 0)
    def _():
        carry_ref[...] = seed_ref[0]

    def x_lru(xv):
        h = _ln_mod(xv, mod)
        return jnp.concatenate([_dot(h, wlo_ref[0]), _dot(h, whi_ref[0])], axis=1)

    xl = x_lru(x_ref[0])
    buf_ref[0:T, :] = xl
    if halo:
        buf_ref[T:T + V7X_SUBLANES, :] = jnp.where(i == 0, 0.0, x_lru(xh_ref[0]))
    else:
        buf_ref[T:T + V7X_SUBLANES, :] = jnp.zeros((V7X_SUBLANES, BRANCH_W), F32)
    xc = cb_ref[0] + cw_ref[0, 0:1, :] * xl
    for k in range(1, LRU_CONV_K):
        xc = xc + cw_ref[0, k:k + 1, :] * buf_ref[k:k + T, :]
    a, u = _rglru_gates(xc, wax_ref, bax_ref, lam_ref)
    h, final = _scan_segments(a, u, carry_ref[...], abuf, ubuf, hbuf, T=T, reverse=True)
    hb_ref[0] = h
    carry_ref[...] = final
    st_ref[0] = final


def _lru_bwd_call(x, mods, P, seed, *, l, T, mod_row):
    B, S, _ = x.shape
    nT = S // T
    halo = nT > 1
    W = BRANCH_W
    d = 2 * l + 1
    rev = lambda b, i: (b, nT - 1 - i, 0)
    lay = lambda b, i: (l, 0, 0)
    dirn = lambda b, i: (d, 0, 0)
    in_specs = [pl.BlockSpec((1, T, D_MODEL), rev)]
    args = [x]
    if halo:
        hb = T // V7X_SUBLANES
        last = S // V7X_SUBLANES - 1
        in_specs.append(pl.BlockSpec((1, V7X_SUBLANES, D_MODEL),
                                     lambda b, i: (b, jnp.minimum((nT - i) * hb, last), 0)))
        args.append(x)
    half = W // 2
    in_specs += [
        pl.BlockSpec((1, V7X_SUBLANES, 3 * D_MODEL), lay),
        pl.BlockSpec((1, D_MODEL, half), lambda b, i: (l, 0, COL_XLRU // half)),
        pl.BlockSpec((1, D_MODEL, half), lambda b, i: (l, 0, COL_XLRU // half + 1)),
        pl.BlockSpec((1, LRU_CONV_K, W), dirn),
        pl.BlockSpec((1, 1, W), dirn),
        pl.BlockSpec((1, W, 2 * W), dirn),
        pl.BlockSpec((1, 1, 2 * W), dirn),
        pl.BlockSpec((1, 1, W), dirn),
        pl.BlockSpec((1, 1, W), lambda b, i: (b, 0, 0)),
    ]
    args += [mods, P["w_in"], P["w_in"], P["lru_cw"], P["lru_cb"], P["lru_wax"], P["lru_bax"], P["lru_lam"], seed]
    return pl.pallas_call(
        functools.partial(_lru_bwd_kernel, T=T, halo=halo, mod_row=mod_row),
        grid=(B, nT),
        in_specs=in_specs,
        out_specs=[pl.BlockSpec((1, T, W), rev), pl.BlockSpec((1, 1, W), lambda b, i: (b, 0, 0))],
        out_shape=[jax.ShapeDtypeStruct((B, S, W), F32), jax.ShapeDtypeStruct((B, 1, W), F32)],
        scratch_shapes=[pltpu.VMEM((T + V7X_SUBLANES, W), F32), pltpu.VMEM((1, W), F32)] + _lru_scratch(T),
        compiler_params=pltpu.CompilerParams(
            dimension_semantics=("arbitrary", "arbitrary"), vmem_limit_bytes=VMEM_LIMIT_BYTES),
        name="lru_bwd",
    )(*args)


def _rope(x, cos, sin_signed, even):
    outs = []
    for xs in _lane_slabs(x):
        up = pltpu.roll(xs, V7X_LANES - HEAD_DIM // 4, axis=1)
        dn = pltpu.roll(xs, HEAD_DIM // 4, axis=1)
        outs.append(xs * cos + jnp.where(even, up, dn) * sin_signed)
    return outs[0] if len(outs) == 1 else jnp.concatenate(outs, axis=1)


def _mix_kernel(*refs, T, S, local, mod_row):
    W = BRANCH_W
    if local:
        (x_ref, xl_ref, xr_ref, cos_ref, cosl_ref, cosr_ref, sin_ref, sinl_ref, sinr_ref, bias_ref,
         kc_ref, vc_ref, hb_ref, mods_ref, win_ref, sink_ref,
         dww_ref, dwb_ref, clg_ref, clb_ref,
         lcw_ref, lcb_ref, wax_ref, bax_ref, lam_ref, seed_ref,
         y_ref, st_ref,
         hext_ref, kext_ref, vext_ref, yatt_ref, cbuf, yext, obuf, lbuf_ref, carry_ref, abuf, ubuf, hbuf) = refs
        halo = BLOCK
    else:
        (x_ref, hb_ref, mods_ref, win_ref, sink_ref,
         dww_ref, dwb_ref, clg_ref, clb_ref,
         lcw_ref, lcb_ref, wax_ref, bax_ref, lam_ref, seed_ref,
         y_ref, st_ref, ko_ref, vo_ref,
         hext_ref, kext_ref, vext_ref, yatt_ref, cbuf, yext, obuf, lbuf_ref, carry_ref, abuf, ubuf, hbuf) = refs
        halo = 0
    i = pl.program_id(1)
    mod = _mod_row(mods_ref, pl.program_id(0) if mod_row is None else mod_row)
    w_in = lambda c0, c1: win_ref[0, :, c0:c1]

    @pl.when(i == 0)
    def _():
        carry_ref[...] = seed_ref[0]

    hext_ref[halo:halo + T, :] = _ln_mod(x_ref[0], mod)
    if local:
        hext_ref[0:halo, :] = _ln_mod(xl_ref[0], mod)
        hext_ref[halo + T:halo + T + halo, :] = _ln_mod(xr_ref[0], mod)
    hc = hext_ref[halo:halo + T, :]

    w_kv = w_in(COL_K, COL_GATT)
    if local:
        lane = lax.broadcasted_iota(jnp.int32, (1, V7X_LANES), 1)
        even = (lane & (HEAD_DIM // 4)) == 0
        parts = ((0, halo, cosl_ref, sinl_ref), (halo, T, cos_ref, sin_ref), (halo + T, halo, cosr_ref, sinr_ref))
        for r0, n, c_ref, s_ref in parts:
            kv = _dot(hext_ref[r0:r0 + n, :], w_kv)
            kext_ref[r0:r0 + n, :] = _rope(kv[:, :KV_W], c_ref[...], s_ref[...], even).astype(BF16)
            vext_ref[r0:r0 + n, :] = kv[:, KV_W:].astype(BF16)
        q = _rope(_dot(hc, w_in(COL_Q, COL_K)), cos_ref[...], sin_ref[...], even) * ATTN_SCALE
    else:
        kv = _dot(hc, w_kv)
        kext_ref[...] = kv[:, :KV_W].astype(BF16)
        vext_ref[...] = kv[:, KV_W:].astype(BF16)
        ko_ref[0] = kext_ref[...]
        vo_ref[0] = vext_ref[...]
        q = _dot(hc, w_in(COL_Q, COL_K)) * ATTN_SCALE

    for jb in range(T // BLOCK):
        r0 = jb * BLOCK
        if local:
            pos0 = i * T + r0
            bidx = jnp.where(pos0 == 0, 1, jnp.where(pos0 == S - BLOCK, 2, 0))
            bias = bias_ref[bidx]
        for hk in range(N_KV_HEADS):
            heads = [GROUP * hk + g for g in range(GROUP)]
            q4 = jnp.concatenate(
                [q[r0:r0 + BLOCK, h * HEAD_DIM:(h + 1) * HEAD_DIM] for h in heads], axis=0).astype(BF16)
            sink = jnp.concatenate(
                [jnp.broadcast_to(sink_ref[0, h:h + 1, 0:1], (BLOCK, 1)) for h in heads], axis=0)
            cs = slice(hk * HEAD_DIM, (hk + 1) * HEAD_DIM)
            if local:
                k_c, v_c = kc_ref[0, :, cs], vc_ref[0, :, cs]
            else:
                k_c, v_c = kext_ref[:, cs], vext_ref[:, cs]
            s_ctx = _dot_nt(q4, k_c)
            m = jnp.maximum(jnp.max(s_ctx, axis=-1, keepdims=True), sink)
            if local:
                k_w = kext_ref[r0:r0 + 3 * BLOCK, cs]
                v_w = vext_ref[r0:r0 + 3 * BLOCK, cs]
                s_loc = _dot_nt(q4, k_w)
                s_loc = jnp.concatenate(
                    [s_loc[g * BLOCK:(g + 1) * BLOCK] + bias for g in range(GROUP)], axis=0)
                m = jnp.maximum(m, jnp.max(s_loc, axis=-1, keepdims=True))
                p_loc = jnp.exp(s_loc - m)
            p_ctx = jnp.exp(s_ctx - m)
            den = jnp.sum(p_ctx, axis=-1, keepdims=True) + jnp.exp(sink - m)
            o = _dot(p_ctx.astype(BF16), v_c)
            if local:
                den = den + jnp.sum(p_loc, axis=-1, keepdims=True)
                o = o + _dot(p_loc.astype(BF16), v_w)
            o = o / den
            for g, h in enumerate(heads):
                yatt_ref[r0:r0 + BLOCK, h * HEAD_DIM:(h + 1) * HEAD_DIM] = o[g * BLOCK:(g + 1) * BLOCK]

    g_att = _dot(hc, w_in(COL_GATT, COL_UCONV))
    y_ref[0, :, 0:W] = (yatt_ref[...] * _silu(g_att)).astype(BF16)

    n_ext = T + 2 * CONV_HALO
    w_cl = w_in(COL_UCONV, COL_GLRU)
    zeros = jnp.zeros((CONV_HALO, W), F32)
    if local:
        ucl = _dot(hext_ref[halo - CONV_HALO:halo + T + CONV_HALO, :], w_cl)
        pos = i * T - CONV_HALO + lax.broadcasted_iota(jnp.int32, (n_ext, 1), 0)
        inside = (pos >= 0) & (pos < S)
        glu = jnp.where(inside, ucl[:, 0:W] * _sigmoid(ucl[:, W:2 * W]), 0.0)
        lbuf_ref[...] = jnp.where(inside, ucl[:, 3 * W:4 * W], 0.0)
        g_conv = ucl[CONV_HALO:CONV_HALO + T, 2 * W:3 * W]
        glu_parts = (glu[CONV_HALO:CONV_HALO + T], glu[CONV_HALO + T:n_ext], glu[0:CONV_HALO])
    else:
        ucl = _dot(hc, w_cl)
        lbuf_ref[0:CONV_HALO, :] = zeros
        lbuf_ref[CONV_HALO + T:n_ext, :] = zeros
        lbuf_ref[CONV_HALO:CONV_HALO + T, :] = ucl[:, 3 * W:4 * W]
        g_conv = ucl[:, 2 * W:3 * W]
        glu_parts = (ucl[:, 0:W] * _sigmoid(ucl[:, W:2 * W]), zeros, zeros)
    for r0, part in zip((0, T, T + CONV_HALO), glu_parts):
        for c, slab in enumerate(_lane_slabs(part)):
            cbuf[c, r0:r0 + part.shape[0], :] = slab

    acc = _dwconv_segments(cbuf, yext, obuf, dww_ref, dwb_ref, T=T, K=CONV_K)
    y_conv = _silu(_layer_norm(acc) * clg_ref[0] + clb_ref[0])
    y_ref[0, :, W:2 * W] = (y_conv * _silu(g_conv)).astype(BF16)

    xc = jnp.broadcast_to(lcb_ref[0], (T, W))
    off = CONV_HALO - (LRU_CONV_K - 1)
    for k in range(LRU_CONV_K):
        xc = xc + lcw_ref[0, k:k + 1, :] * lbuf_ref[off + k:off + k + T, :]
    a, u = _rglru_gates(xc, wax_ref, bax_ref, lam_ref)
    hf, final = _scan_segments(a, u, carry_ref[...], abuf, ubuf, hbuf, T=T, reverse=False)
    carry_ref[...] = final
    st_ref[0] = final
    g_lru = _dot(hc, w_in(COL_GLRU, IN_COLS))
    y_ref[0, :, 2 * W:3 * W] = ((hf + hb_ref[0]) * _silu(g_lru)).astype(BF16)


def _mix_call(x, hb, mods, P, seed, *, l, T, local, mod_row, rope=None, bias=None, kc=None, vc=None):
    B, S, _ = x.shape
    nT = S // T
    W = BRANCH_W
    tile = lambda b, i: (b, i, 0)
    batch = lambda b, i: (b, 0, 0)
    lay = lambda b, i: (l, 0, 0)
    dirn = lambda b, i: (2 * l, 0, 0)
    in_specs, args = [pl.BlockSpec((1, T, D_MODEL), tile)], [x]
    if local:
        nb = T // BLOCK
        last = S // BLOCK - 1
        left = lambda b, i: (b, jnp.maximum(i * nb - 1, 0), 0)
        right = lambda b, i: (b, jnp.minimum((i + 1) * nb, last), 0)
        in_specs += [pl.BlockSpec((1, BLOCK, D_MODEL), left), pl.BlockSpec((1, BLOCK, D_MODEL), right)]
        args += [x, x]
        cos, sin = rope
        for tab in (cos, sin):
            in_specs += [
                pl.BlockSpec((T, V7X_LANES), lambda b, i: (i, 0)),
                pl.BlockSpec((BLOCK, V7X_LANES), lambda b, i: (jnp.maximum(i * nb - 1, 0), 0)),
                pl.BlockSpec((BLOCK, V7X_LANES), lambda b, i: (jnp.minimum((i + 1) * nb, last), 0)),
            ]
            args += [tab, tab, tab]
        C = kc.shape[1]
        in_specs += [
            pl.BlockSpec((3, BLOCK, 3 * BLOCK), lambda b, i: (0, 0, 0)),
            pl.BlockSpec((1, C, KV_W), batch),
            pl.BlockSpec((1, C, KV_W), batch),
        ]
        args += [bias, kc, vc]
    in_specs += [
        pl.BlockSpec((1, T, W), tile),
        pl.BlockSpec((1, V7X_SUBLANES, 3 * D_MODEL), lay),
        pl.BlockSpec((1, D_MODEL, IN_COLS), lay),
        pl.BlockSpec((1, N_HEADS, V7X_LANES), lay),
        pl.BlockSpec((1, CONV_K, W), lay),
        pl.BlockSpec((1, 1, W), lay),
        pl.BlockSpec((1, 1, W), lay),
        pl.BlockSpec((1, 1, W), lay),
        pl.BlockSpec((1, LRU_CONV_K, W), dirn),
        pl.BlockSpec((1, 1, W), dirn),
        pl.BlockSpec((1, W, 2 * W), dirn),
        pl.BlockSpec((1, 1, 2 * W), dirn),
        pl.BlockSpec((1, 1, W), dirn),
        pl.BlockSpec((1, 1, W), batch),
    ]
    args += [hb, mods, P["w_in"], P["sink"], P["dw_w"], P["dw_b"], P["cl_g"], P["cl_b"],
             P["lru_cw"], P["lru_cb"], P["lru_wax"], P["lru_bax"], P["lru_lam"], seed]
    out_specs = [pl.BlockSpec((1, T, 3 * W), tile), pl.BlockSpec((1, 1, W), batch)]
    out_shape = [jax.ShapeDtypeStruct((B, S, 3 * W), BF16), jax.ShapeDtypeStruct((B, 1, W), F32)]
    if not local:
        out_specs += [pl.BlockSpec((1, T, KV_W), tile), pl.BlockSpec((1, T, KV_W), tile)]
        out_shape += [jax.ShapeDtypeStruct((B, S, KV_W), BF16), jax.ShapeDtypeStruct((B, S, KV_W), BF16)]
    halo = BLOCK if local else 0
    n_ext = T + 2 * CONV_HALO
    L = _seg_len(T)
    scratch = [
        pltpu.VMEM((T + 2 * halo, D_MODEL), BF16),
        pltpu.VMEM((T + 2 * halo, KV_W), BF16),
        pltpu.VMEM((T + 2 * halo, KV_W), BF16),
        pltpu.VMEM((T, W), F32),
        pltpu.VMEM((N_SLAB, n_ext, V7X_LANES), F32),
        pltpu.VMEM((N_SLAB, L + 2 * (CONV_K // 2), V7X_SUBLANES, V7X_LANES), F32),
        pltpu.VMEM((N_SLAB, n_ext, V7X_LANES), F32),
        pltpu.VMEM((n_ext, W), F32),
        pltpu.VMEM((1, W), F32),
    ] + _lru_scratch(T)
    return pl.pallas_call(
        functools.partial(_mix_kernel, T=T, S=S, local=local, mod_row=mod_row),
        grid=(B, nT),
        in_specs=in_specs,
        out_specs=out_specs,
        out_shape=out_shape,
        scratch_shapes=scratch,
        compiler_params=pltpu.CompilerParams(
            dimension_semantics=("arbitrary", "arbitrary"), vmem_limit_bytes=VMEM_LIMIT_BYTES),
        name="mix_local" if local else "mix_ctx",
    )(*args)


def _merge_kernel(x_ref, y_ref, mods_ref, wgate_ref, bgate_ref, wbr_ref, wout_ref, lng_ref, lnb_ref, o_ref, *,
                  mod_row):
    x = x_ref[0]
    mod = _mod_row(mods_ref, pl.program_id(0) if mod_row is None else mod_row)
    h = _ln_mod(x, mod)
    m = None
    for n in range(N_BRANCH):
        cols = slice(n * D_MODEL, (n + 1) * D_MODEL)
        g = _sigmoid(_dot(h, wgate_ref[0, :, cols]) + bgate_ref[0, :, cols])
        p = _dot(y_ref[0, :, n * BRANCH_W:(n + 1) * BRANCH_W], wbr_ref[0, n])
        m = p * g if m is None else m + p * g
    out = _dot(m.astype(BF16), wout_ref[0])
    gate = mod[:, 2 * D_MODEL:3 * D_MODEL]
    z = DEEPNORM_ALPHA * x + gate * out
    o_ref[0] = _layer_norm(z) * lng_ref[0] + lnb_ref[0]


def _merge_call(x, y, mods, P, *, l, T, mod_row):
    B, S, _ = x.shape
    nT = S // T
    tile = lambda b, i: (b, i, 0)
    lay = lambda b, i: (l, 0, 0)
    return pl.pallas_call(
        functools.partial(_merge_kernel, mod_row=mod_row),
        grid=(B, nT),
        in_specs=[
            pl.BlockSpec((1, T, D_MODEL), tile),
            pl.BlockSpec((1, T, 3 * BRANCH_W), tile),
            pl.BlockSpec((1, V7X_SUBLANES, 3 * D_MODEL), lay),
            pl.BlockSpec((1, D_MODEL, 3 * D_MODEL), lay),
            pl.BlockSpec((1, 1, 3 * D_MODEL), lay),
            pl.BlockSpec((1, N_BRANCH, BRANCH_W, D_MODEL), lambda b, i: (l, 0, 0, 0)),
            pl.BlockSpec((1, D_MODEL, D_MODEL), lay),
            pl.BlockSpec((1, 1, D_MODEL), lay),
            pl.BlockSpec((1, 1, D_MODEL), lay),
        ],
        out_specs=pl.BlockSpec((1, T, D_MODEL), tile),
        out_shape=jax.ShapeDtypeStruct((B, S, D_MODEL), F32),
        compiler_params=pltpu.CompilerParams(
            dimension_semantics=("arbitrary", "arbitrary"), vmem_limit_bytes=VMEM_LIMIT_BYTES),
        name="merge",
    )(x, y, mods, P["w_gate"], P["b_gate"], P["w_branch"], P["w_out"], P["ln_g"], P["ln_b"])


def _rope_tables(S):
    quarter = HEAD_DIM // 4
    inv = ROPE_BASE ** (-jnp.arange(quarter, dtype=F32) / quarter)
    pos = jnp.arange(S)
    ang_r = (pos // GRID_W).astype(F32)[:, None] * inv
    ang_c = (pos % GRID_W).astype(F32)[:, None] * inv
    ang = jnp.concatenate([ang_r, ang_r, ang_c, ang_c] * 2, axis=-1)
    sign = jnp.tile(jnp.concatenate([-jnp.ones(quarter, F32), jnp.ones(quarter, F32)]), 2 * V7X_LANES // HEAD_DIM)
    return jnp.cos(ang), jnp.sin(ang) * sign


def _window_bias():
    r = np.arange(BLOCK)[:, None]
    j = np.arange(3 * BLOCK)[None, :]
    band = np.abs(j - BLOCK - r) <= WINDOW
    first = band & (j >= BLOCK)
    last = band & (j < 2 * BLOCK)
    return jnp.asarray(np.where(np.stack([band, first, last]), 0.0, -np.inf), dtype=F32)


def _block_diag(w):
    n, d, e = w.shape[-3:]
    eye = jnp.eye(n, dtype=w.dtype)
    dense = eye[:, None, :, None] * w[..., :, :, None, :]
    return dense.reshape(w.shape[:-3] + (n * d, n * e))


def _pack_params(w_in, attn_sink, conv_dw_w, conv_dw_b, conv_ln_g, conv_ln_b, lru_conv_w, lru_conv_b,
                 lru_w_a, lru_b_a, lru_w_x, lru_b_x, lru_lambda, w_branch, w_gate, b_gate, w_out, ln_g, ln_b):
    depth = w_in.shape[0]
    W = BRANCH_W
    rows = lambda v: v.reshape(-1, 1, v.shape[-1])
    return {
        "w_in": w_in.astype(BF16),
        "sink": jnp.broadcast_to(attn_sink[:, :, None], (depth, N_HEADS, V7X_LANES)),
        "dw_w": conv_dw_w, "dw_b": rows(conv_dw_b), "cl_g": rows(conv_ln_g), "cl_b": rows(conv_ln_b),
        "lru_cw": lru_conv_w.reshape(2 * depth, LRU_CONV_K, W),
        "lru_cb": rows(lru_conv_b),
        "lru_wax": jnp.concatenate([_block_diag(lru_w_a), _block_diag(lru_w_x)], axis=-1)
                      .astype(BF16).reshape(2 * depth, W, 2 * W),
        "lru_bax": rows(jnp.concatenate([lru_b_a, lru_b_x], axis=-1)),
        "lru_lam": rows(lru_lambda),
        "w_gate": w_gate.astype(BF16), "b_gate": rows(b_gate),
        "w_branch": w_branch.astype(BF16), "w_out": w_out.astype(BF16),
        "ln_g": rows(ln_g), "ln_b": rows(ln_b),
    }


TILE_T = 512


def kernel(x, c, ctx, c_ctx, w_ada, b_ada, w_in, attn_sink, conv_dw_w, conv_dw_b, conv_ln_g, conv_ln_b,
           lru_conv_w, lru_conv_b, lru_w_a, lru_b_a, lru_w_x, lru_b_x, lru_lambda,
           w_branch, w_gate, b_gate, w_out, ln_g, ln_b):
    B, S, D = x.shape
    C = ctx.shape[1]
    depth = w_ada.shape[0]
    assert D == D_MODEL and S % TILE_T == 0 and S >= 2 * BLOCK and B + 1 <= V7X_SUBLANES

    c_rows = jnp.concatenate([c, c_ctx[None, :], jnp.zeros((V7X_SUBLANES - B - 1, D), F32)], axis=0)
    mods = _mods_call(c_rows, w_ada, b_ada)
    P = _pack_params(w_in, attn_sink, conv_dw_w, conv_dw_b, conv_ln_g, conv_ln_b, lru_conv_w, lru_conv_b,
                     lru_w_a, lru_b_a, lru_w_x, lru_b_x, lru_lambda, w_branch, w_gate, b_gate, w_out, ln_g, ln_b)
    rope = _rope_tables(S)
    bias = _window_bias()
    zero_seed = jnp.zeros((B, 1, BRANCH_W), F32)

    for l in range(depth):
        hb_c, seed_b = _lru_bwd_call(ctx, mods, P, zero_seed, l=l, T=C, mod_row=B)
        y_c, seed_f, kc, vc = _mix_call(ctx, hb_c, mods, P, zero_seed, l=l, T=C, local=False, mod_row=B)

        hb = _lru_bwd_call(x, mods, P, seed_b, l=l, T=TILE_T, mod_row=None)[0]
        y = _mix_call(x, hb, mods, P, seed_f, l=l, T=TILE_T, local=True, mod_row=None,
                      rope=rope, bias=bias, kc=kc, vc=vc)[0]
        x = _merge_call(x, y, mods, P, l=l, T=TILE_T, mod_row=None)
        if l != depth - 1:
            ctx = _merge_call(ctx, y_c, mods, P, l=l, T=C, mod_row=B)
    return x
```

```python
import functools

import jax
import jax.numpy as jnp
import numpy as np
from jax import lax
from jax.experimental import pallas as pl
from jax.experimental.pallas import tpu as pltpu

F32 = jnp.float32
BF16 = jnp.bfloat16

D_MODEL = 1024
DEPTH = 4
GRID_W = 64
BRANCH_W = D_MODEL // 2
N_BRANCH = 3
HEAD_DIM = 64
N_HEADS = BRANCH_W // HEAD_DIM
N_KV_HEADS = N_HEADS // 4
GROUP = N_HEADS // N_KV_HEADS
KV_W = N_KV_HEADS * HEAD_DIM
V_EXT_W = N_KV_HEADS * 128
WINDOW = 128
BLOCK = 128
ROPE_BASE = 10000.0
CONV_K = 31
CONV_HALO = 16
LRU_BLOCKS = 8
LRU_CONV_K = 4
LRU_C = 8.0
DEEPNORM_ALPHA = (2 * DEPTH) ** 0.25
LN_EPS = 1e-6
ATTN_SCALE = HEAD_DIM ** -0.5

_IN_SIZES = (N_HEADS * HEAD_DIM, KV_W, KV_W, BRANCH_W, 2 * BRANCH_W, BRANCH_W, BRANCH_W, BRANCH_W)
_IN_OFF = np.cumsum((0,) + _IN_SIZES).tolist()
IN_COLS = _IN_OFF[-1]
COL_Q, COL_K, COL_V, COL_GATT, COL_UCONV, COL_GCONV, COL_XLRU, COL_GLRU = _IN_OFF[:-1]

V7X_LANES = 128
V7X_SUBLANES = 8
VMEM_LIMIT_BYTES = 56 * 1024 * 1024
N_SLAB = BRANCH_W // V7X_LANES
MXU_TILE = 256
LRU_TILES = BRANCH_W // MXU_TILE
LOG2_E = 1.4426950408889634


def _seg_len(T):
    L = (T + 2 * CONV_HALO) // V7X_SUBLANES
    assert V7X_SUBLANES * L == T + 2 * CONV_HALO and L % 8 == 4, (T, L)
    return L


def _layer_norm(x):
    mu = jnp.mean(x, axis=-1, keepdims=True)
    xc = x - mu
    var = jnp.mean(xc * xc, axis=-1, keepdims=True)
    return xc * lax.rsqrt(var + LN_EPS)


def _sigmoid(x):
    return jax.nn.sigmoid(x)


def _silu(x):
    return x * jax.nn.sigmoid(x)


def _dot(a, b):
    return jnp.dot(a, b, preferred_element_type=F32)


def _dot_nt(a, b):
    return lax.dot_general(a, b, (((1,), (1,)), ((), ())), preferred_element_type=F32)


def _ln_mod(x, mod):
    shift = mod[:, :D_MODEL]
    scale = mod[:, D_MODEL:2 * D_MODEL]
    return (_layer_norm(x) * (1.0 + scale) + shift).astype(BF16)


def _mod_row(mods_ref, row):
    if isinstance(row, int):
        return mods_ref[0, row:row + 1, :]
    return mods_ref[0, pl.ds(row, 1), :]


def _lane_slabs(x):
    return [x[:, c * V7X_LANES:(c + 1) * V7X_LANES] for c in range(x.shape[1] // V7X_LANES)]


def _mods_kernel(c_ref, w_ref, b_ref, o_ref):
    s = _silu(c_ref[...])
    o_ref[0] = _dot(s.astype(BF16), w_ref[0].astype(BF16)) + b_ref[0]


def _mods_call(c_rows, w_ada, b_ada):
    depth = w_ada.shape[0]
    ncol = 3
    return pl.pallas_call(
        _mods_kernel,
        grid=(depth, ncol),
        in_specs=[
            pl.BlockSpec((V7X_SUBLANES, D_MODEL), lambda l, j: (0, 0)),
            pl.BlockSpec((1, D_MODEL, D_MODEL), lambda l, j: (l, 0, j)),
            pl.BlockSpec((1, 1, D_MODEL), lambda l, j: (l, 0, j)),
        ],
        out_specs=pl.BlockSpec((1, V7X_SUBLANES, D_MODEL), lambda l, j: (l, 0, j)),
        out_shape=jax.ShapeDtypeStruct((depth, V7X_SUBLANES, 3 * D_MODEL), F32),
        compiler_params=pltpu.CompilerParams(
            dimension_semantics=("arbitrary", "arbitrary"), vmem_limit_bytes=VMEM_LIMIT_BYTES),
        name="adaln_mods",
    )(c_rows, w_ada, b_ada.reshape(depth, 1, 3 * D_MODEL))


def _rglru_gates(xc, wax_ref, bax_ref, lam_ref):
    xb = xc.astype(BF16)
    cols = [xb[:, t * MXU_TILE:(t + 1) * MXU_TILE] for t in range(LRU_TILES)]
    gate = lambda g: jnp.concatenate(
        [_dot(cols[t], wax_ref[0, g * LRU_TILES + t]) for t in range(LRU_TILES)], axis=1)
    r = _sigmoid(gate(0) + bax_ref[0, :, :BRANCH_W])
    ig = _sigmoid(gate(1) + bax_ref[0, :, BRANCH_W:])
    z = -lam_ref[0]
    softplus = jnp.maximum(z, 0.0) + jnp.log1p(jnp.exp(-jnp.abs(z)))
    log_a = (-LRU_C) * r * softplus
    a = jnp.exp(log_a)
    u = jnp.sqrt(jnp.tanh(-log_a) * (a * a + 1.0)) * ig * xc
    return a, u


def _scan_tasks(a, u, carry, abuf, ubuf, hbuf, *, T, reverse):
    L = _seg_len(T)
    rows = V7X_SUBLANES * L
    for c, (a_c, u_c) in enumerate(zip(_lane_slabs(a), _lane_slabs(u))):
        abuf[c, 0:T, :] = a_c
        ubuf[c, 0:T, :] = u_c
        abuf[c, T:rows, :] = jnp.ones((rows - T, V7X_LANES), F32)
        ubuf[c, T:rows, :] = jnp.zeros((rows - T, V7X_LANES), F32)
    order = range(L - 1, -1, -1) if reverse else range(L)
    seg = lambda buf, c, j: buf[c, pl.ds(j, V7X_SUBLANES, stride=L), :]
    finals = [None] * N_SLAB

    def slab_task(c, cin):
        sub = lax.broadcasted_iota(jnp.int32, (V7X_SUBLANES, V7X_LANES), 0)
        h = jnp.zeros((V7X_SUBLANES, V7X_LANES), F32)
        p = jnp.ones((V7X_SUBLANES, V7X_LANES), F32)
        for j in order:
            a_j = seg(abuf, c, j)
            h = a_j * h + seg(ubuf, c, j)
            p = a_j * p
        for d in (1, 2, 4):
            sh = V7X_SUBLANES - d if reverse else d
            m = (sub < V7X_SUBLANES - d) if reverse else (sub >= d)
            h = jnp.where(m, p * pltpu.roll(h, sh, axis=0) + h, h)
            p = jnp.where(m, p * pltpu.roll(p, sh, axis=0), p)
        end = h + p * cin
        if reverse:
            h = jnp.where(sub == V7X_SUBLANES - 1, cin, pltpu.roll(end, V7X_SUBLANES - 1, axis=0))
            finals[c] = end[0:1, :]
        else:
            h = jnp.where(sub == 0, cin, pltpu.roll(end, 1, axis=0))
            finals[c] = end[V7X_SUBLANES - 1:V7X_SUBLANES, :]
        for j in order:
            h = seg(abuf, c, j) * h + seg(ubuf, c, j)
            hbuf[c, pl.ds(j, V7X_SUBLANES, stride=L), :] = h

    tasks = [functools.partial(slab_task, c, cin) for c, cin in enumerate(_lane_slabs(carry))]

    def finish():
        h_all = jnp.concatenate([hbuf[c, 0:T, :] for c in range(N_SLAB)], axis=1)
        return h_all, jnp.concatenate(finals, axis=1)

    return tasks, finish


def _scan_segments(a, u, carry, abuf, ubuf, hbuf, *, T, reverse):
    tasks, finish = _scan_tasks(a, u, carry, abuf, ubuf, hbuf, T=T, reverse=reverse)
    for task in tasks:
        task()
    return finish()


CONV_SEGMENTS = 2 * V7X_SUBLANES


def _conv_seg_len(T):
    L = (T + 2 * CONV_HALO) // CONV_SEGMENTS
    assert CONV_SEGMENTS * L == T + 2 * CONV_HALO and L % 8 != 0 and L >= CONV_K // 2, (T, L)
    return L


def _dwconv_tasks(ybuf, yext, wext, obuf, w_ref, b_ref, *, T, K, chunks):
    L = _conv_seg_len(T)
    half = K // 2
    hi = V7X_SUBLANES * L

    def stage(c):
        lanes = slice(c * V7X_LANES, (c + 1) * V7X_LANES)
        for k in range(K):
            wext[c, k] = jnp.broadcast_to(w_ref[0, k:k + 1, lanes], (CONV_SEGMENTS, V7X_LANES)).astype(BF16)
        for j in range(L):
            v = jnp.concatenate([ybuf[c, pl.ds(j, V7X_SUBLANES, stride=L), :],
                                 ybuf[c, pl.ds(hi + j, V7X_SUBLANES, stride=L), :]], axis=0)
            yext[c, half + j] = v.astype(BF16)
            if j >= L - half:
                yext[c, half + j - L] = pltpu.roll(v, 1, axis=0).astype(BF16)
            if j < half:
                yext[c, half + j + L] = pltpu.roll(v, CONV_SEGMENTS - 1, axis=0).astype(BF16)

    def taps(c, j0, j1):
        lanes = slice(c * V7X_LANES, (c + 1) * V7X_LANES)
        bc = jnp.broadcast_to(b_ref[0, :, lanes], (CONV_SEGMENTS, V7X_LANES))
        for j in range(j0, j1):
            acc = jnp.zeros((CONV_SEGMENTS, V7X_LANES), F32)
            for k in range(K):
                acc = acc + yext[c, j + k].astype(F32) * wext[c, k].astype(F32)
            acc = acc + bc
            obuf[c, pl.ds(j, V7X_SUBLANES, stride=L), :] = acc[0:V7X_SUBLANES]
            obuf[c, pl.ds(hi + j, V7X_SUBLANES, stride=L), :] = acc[V7X_SUBLANES:]

    step = -(-L // chunks)
    tasks = [functools.partial(stage, c) for c in range(N_SLAB)]
    for c in range(N_SLAB):
        tasks += [functools.partial(taps, c, j0, min(j0 + step, L)) for j0 in range(0, L, step)]

    def finish():
        return jnp.concatenate([obuf[c, 0:T, :] for c in range(N_SLAB)], axis=1)

    return tasks, finish


def _lru_scratch(T):
    rows = V7X_SUBLANES * _seg_len(T)
    return [pltpu.VMEM((N_SLAB, rows, V7X_LANES), F32) for _ in range(3)]


def _lru_bwd_kernel(*refs, T, halo, mod_row):
    if halo:
        (x_ref, xh_ref, mods_ref, wlo_ref, whi_ref, cw_ref, cb_ref, wax_ref, bax_ref, lam_ref, seed_ref,
         hb_ref, st_ref, buf_ref, carry_ref, abuf, ubuf, hbuf) = refs
    else:
        (x_ref, mods_ref, wlo_ref, whi_ref, cw_ref, cb_ref, wax_ref, bax_ref, lam_ref, seed_ref,
         hb_ref, st_ref, buf_ref, carry_ref, abuf, ubuf, hbuf) = refs
    i = pl.program_id(1)
    mod = _mod_row(mods_ref, pl.program_id(0) if mod_row is None else mod_row)

    @pl.when(i == 0)
    def _():
        carry_ref[...] = seed_ref[0]

    def x_lru(xv):
        h = _ln_mod(xv, mod)
        return jnp.concatenate([_dot(h, wlo_ref[0]), _dot(h, whi_ref[0])], axis=1)

    xl = x_lru(x_ref[0])
    buf_ref[0:T, :] = xl
    if halo:
        buf_ref[T:T + V7X_SUBLANES, :] = jnp.where(i == 0, 0.0, x_lru(xh_ref[0]))
    else:
        buf_ref[T:T + V7X_SUBLANES, :] = jnp.zeros((V7X_SUBLANES, BRANCH_W), F32)
    xc = cb_ref[0] + cw_ref[0, 0:1, :] * xl
    for k in range(1, LRU_CONV_K):
        xc = xc + cw_ref[0, k:k + 1, :] * buf_ref[k:k + T, :]
    a, u = _rglru_gates(xc, wax_ref, bax_ref, lam_ref)
    h, final = _scan_segments(a, u, carry_ref[...], abuf, ubuf, hbuf, T=T, reverse=True)
    hb_ref[0] = h
    carry_ref[...] = final
    st_ref[0] = final


def _lru_bwd_call(x, mods, P, seed, *, l, T, mod_row):
    B, S, _ = x.shape
    nT = S // T
    halo = nT > 1
    W = BRANCH_W
    d = 2 * l + 1
    rev = lambda b, i: (b, nT - 1 - i, 0)
    lay = lambda b, i: (l, 0, 0)
    dirn = lambda b, i: (d, 0, 0)
    in_specs = [pl.BlockSpec((1, T, D_MODEL), rev)]
    args = [x]
    if halo:
        hb = T // V7X_SUBLANES
        last = S // V7X_SUBLANES - 1
        in_specs.append(pl.BlockSpec((1, V7X_SUBLANES, D_MODEL),
                                     lambda b, i: (b, jnp.minimum((nT - i) * hb, last), 0)))
        args.append(x)
    half = W // 2
    in_specs += [
        pl.BlockSpec((1, V7X_SUBLANES, 3 * D_MODEL), lay),
        pl.BlockSpec((1, D_MODEL, half), lambda b, i: (l, 0, COL_XLRU // half)),
        pl.BlockSpec((1, D_MODEL, half), lambda b, i: (l, 0, COL_XLRU // half + 1)),
        pl.BlockSpec((1, LRU_CONV_K, W), dirn),
        pl.BlockSpec((1, 1, W), dirn),
        pl.BlockSpec((1, 2 * LRU_TILES, MXU_TILE, MXU_TILE), lambda b, i, d=dirn: d(b, i) + (0,)),
        pl.BlockSpec((1, 1, 2 * W), dirn),
        pl.BlockSpec((1, 1, W), dirn),
        pl.BlockSpec((1, 1, W), lambda b, i: (b, 0, 0)),
    ]
    args += [mods, P["w_in"], P["w_in"], P["lru_cw"], P["lru_cb"], P["lru_wax"], P["lru_bax"], P["lru_lam"], seed]
    return pl.pallas_call(
        functools.partial(_lru_bwd_kernel, T=T, halo=halo, mod_row=mod_row),
        grid=(B, nT),
        in_specs=in_specs,
        out_specs=[pl.BlockSpec((1, T, W), rev), pl.BlockSpec((1, 1, W), lambda b, i: (b, 0, 0))],
        out_shape=[jax.ShapeDtypeStruct((B, S, W), F32), jax.ShapeDtypeStruct((B, 1, W), F32)],
        scratch_shapes=[pltpu.VMEM((T + V7X_SUBLANES, W), F32), pltpu.VMEM((1, W), F32)] + _lru_scratch(T),
        compiler_params=pltpu.CompilerParams(
            dimension_semantics=("arbitrary", "arbitrary"), vmem_limit_bytes=VMEM_LIMIT_BYTES),
        name="lru_bwd",
    )(*args)


def _rope(x, cos, sin_signed, even):
    outs = []
    for xs in _lane_slabs(x):
        up = pltpu.roll(xs, V7X_LANES - HEAD_DIM // 4, axis=1)
        dn = pltpu.roll(xs, HEAD_DIM // 4, axis=1)
        outs.append(xs * cos + jnp.where(even, up, dn) * sin_signed)
    return outs[0] if len(outs) == 1 else jnp.concatenate(outs, axis=1)


def _mix_kernel(*refs, T, S, local, mod_row):
    W = BRANCH_W
    if local:
        (x_ref, xl_ref, xr_ref, cos_ref, cosl_ref, cosr_ref, sin_ref, sinl_ref, sinr_ref, bias_ref,
         kc_ref, vc_ref, hb_ref, mods_ref, win_ref, sink_ref,
         dww_ref, dwb_ref, clg_ref, clb_ref,
         lcw_ref, lcb_ref, wax_ref, bax_ref, lam_ref, seed_ref,
         y_ref, st_ref,
         hext_ref, kext_ref, vext_ref, yatt_ref, cbuf, yext, wext, obuf, lbuf_ref, carry_ref,
         abuf, ubuf, hbuf) = refs
        halo = BLOCK
    else:
        (x_ref, hb_ref, mods_ref, win_ref, sink_ref,
         dww_ref, dwb_ref, clg_ref, clb_ref,
         lcw_ref, lcb_ref, wax_ref, bax_ref, lam_ref, seed_ref,
         y_ref, st_ref, ko_ref, vo_ref,
         hext_ref, kext_ref, vext_ref, yatt_ref, cbuf, yext, wext, obuf, lbuf_ref, carry_ref,
         abuf, ubuf, hbuf) = refs
        halo = 0
    i = pl.program_id(1)
    mod = _mod_row(mods_ref, pl.program_id(0) if mod_row is None else mod_row)
    w_in = lambda c0, c1: win_ref[0, :, c0:c1]

    @pl.when(i == 0)
    def _():
        carry_ref[...] = seed_ref[0]

    hext_ref[halo:halo + T, :] = _ln_mod(x_ref[0], mod)
    if local:
        hext_ref[0:halo, :] = _ln_mod(xl_ref[0], mod)
        hext_ref[halo + T:halo + T + halo, :] = _ln_mod(xr_ref[0], mod)
    hc = hext_ref[halo:halo + T, :]

    state = {}
    if local:
        lane = lax.broadcasted_iota(jnp.int32, (1, V7X_LANES), 1)
        even = (lane & (HEAD_DIM // 4)) == 0

    def with_ones(v):
        ones = jnp.ones((v.shape[0], HEAD_DIM), F32)
        return jnp.concatenate([v[:, :HEAD_DIM], ones, v[:, HEAD_DIM:], ones], axis=1).astype(BF16)

    def proj_kv():
        w_kv = w_in(COL_K, COL_GATT)
        if local:
            parts = ((0, halo, cosl_ref, sinl_ref), (halo, T, cos_ref, sin_ref),
                     (halo + T, halo, cosr_ref, sinr_ref))
            for r0, n, c_ref, s_ref in parts:
                kv = _dot(hext_ref[r0:r0 + n, :], w_kv)
                kext_ref[r0:r0 + n, :] = _rope(kv[:, :KV_W], c_ref[...], s_ref[...], even).astype(BF16)
                vext_ref[r0:r0 + n, :] = with_ones(kv[:, KV_W:])
        else:
            kv = _dot(hc, w_kv)
            kext_ref[...] = kv[:, :KV_W].astype(BF16)
            vext_ref[...] = with_ones(kv[:, KV_W:])
            ko_ref[0] = kext_ref[...]
            vo_ref[0] = vext_ref[...]

    def proj_q():
        q = _dot(hc, w_in(COL_Q, COL_K))
        if local:
            q = _rope(q, cos_ref[...], sin_ref[...], even)
        state["q"] = q * (ATTN_SCALE * LOG2_E)

    def attention(jb, hk):
        q = state["q"]
        r0 = jb * BLOCK
        heads = [GROUP * hk + g for g in range(GROUP)]
        q4 = jnp.concatenate(
            [q[r0:r0 + BLOCK, h * HEAD_DIM:(h + 1) * HEAD_DIM] for h in heads], axis=0).astype(BF16)
        sink = LOG2_E * jnp.concatenate(
            [jnp.broadcast_to(sink_ref[0, h:h + 1, 0:1], (BLOCK, 1)) for h in heads], axis=0)
        ks = slice(hk * HEAD_DIM, (hk + 1) * HEAD_DIM)
        vs = slice(hk * V7X_LANES, (hk + 1) * V7X_LANES)
        if local:
            k_c, v_c = kc_ref[0, :, ks], vc_ref[0, :, vs]
        else:
            k_c, v_c = kext_ref[:, ks], vext_ref[:, vs]
        s_ctx = _dot_nt(q4, k_c)
        m = jnp.maximum(jnp.max(s_ctx, axis=-1, keepdims=True), sink)
        if local:
            pos0 = i * T + r0
            bidx = jnp.where(pos0 == 0, 1, jnp.where(pos0 == S - BLOCK, 2, 0))
            bias = bias_ref[bidx]
            k_w = kext_ref[r0:r0 + 3 * BLOCK, ks]
            v_w = vext_ref[r0:r0 + 3 * BLOCK, vs]
            s_loc = _dot_nt(q4, k_w)
            s_loc = jnp.concatenate(
                [s_loc[g * BLOCK:(g + 1) * BLOCK] + bias for g in range(GROUP)], axis=0)
            m = jnp.maximum(m, jnp.max(s_loc, axis=-1, keepdims=True))
        o = _dot(jnp.exp2(s_ctx - m).astype(BF16), v_c)
        if local:
            o = o + _dot(jnp.exp2(s_loc - m).astype(BF16), v_w)
        den = pltpu.roll(o, HEAD_DIM, axis=1) + jnp.exp2(sink - m)
        o = o / den
        for g, h in enumerate(heads):
            yatt_ref[r0:r0 + BLOCK, h * HEAD_DIM:(h + 1) * HEAD_DIM] = o[g * BLOCK:(g + 1) * BLOCK, :HEAD_DIM]

    n_ext = T + 2 * CONV_HALO
    zeros = jnp.zeros((CONV_HALO, W), F32)
    h_wide = hext_ref[halo - CONV_HALO:halo + T + CONV_HALO, :] if local else hc
    if local:
        pos = i * T - CONV_HALO + lax.broadcasted_iota(jnp.int32, (n_ext, 1), 0)
        inside = (pos >= 0) & (pos < S)
    conv_tasks, conv_finish = _dwconv_tasks(cbuf, yext, wext, obuf, dww_ref, dwb_ref, T=T, K=CONV_K, chunks=2)
    lru_tasks, lru_finish = [], []

    def proj_glu():
        ag = _dot(h_wide, w_in(COL_UCONV, COL_GCONV))
        glu = ag[:, 0:W] * _sigmoid(ag[:, W:2 * W])
        if local:
            glu = jnp.where(inside, glu, 0.0)
            parts = (glu[CONV_HALO:CONV_HALO + T], glu[CONV_HALO + T:n_ext], glu[0:CONV_HALO])
        else:
            parts = (glu, zeros, zeros)
        for r0, part in zip((0, T, T + CONV_HALO), parts):
            for c, slab in enumerate(_lane_slabs(part)):
                cbuf[c, r0:r0 + part.shape[0], :] = slab
        side.append(conv_tasks)

    def lru_front():
        xc = jnp.broadcast_to(lcb_ref[0], (T, W))
        off = CONV_HALO - (LRU_CONV_K - 1)
        for k in range(LRU_CONV_K):
            xc = xc + lcw_ref[0, k:k + 1, :] * lbuf_ref[off + k:off + k + T, :]
        a, u = _rglru_gates(xc, wax_ref, bax_ref, lam_ref)
        tasks, finish = _scan_tasks(a, u, carry_ref[...], abuf, ubuf, hbuf, T=T, reverse=False)
        lru_tasks.extend(tasks)
        lru_finish.append(finish)

    def proj_gate_lru():
        gx = _dot(h_wide, w_in(COL_GCONV, COL_GLRU))
        if local:
            lbuf_ref[...] = jnp.where(inside, gx[:, W:2 * W], 0.0)
            state["g_conv"] = gx[CONV_HALO:CONV_HALO + T, 0:W]
        else:
            lbuf_ref[0:CONV_HALO, :] = zeros
            lbuf_ref[CONV_HALO + T:n_ext, :] = zeros
            lbuf_ref[CONV_HALO:CONV_HALO + T, :] = gx[:, W:2 * W]
            state["g_conv"] = gx[:, 0:W]
        lru_tasks.append(lru_front)
        side.append(lru_tasks)

    side = []
    for proj in (proj_kv, proj_q, proj_glu, proj_gate_lru):
        proj()
    blocks = [(jb, hk) for jb in range(T // BLOCK) for hk in range(N_KV_HEADS)]
    for n, (jb, hk) in enumerate(blocks):
        attention(jb, hk)
        left = len(blocks) - n
        for queue in side:
            for _ in range(-(-len(queue) // left)):
                queue.pop(0)()
    assert not conv_tasks and not lru_tasks
    g_conv = state["g_conv"]

    g_att = _dot(hc, w_in(COL_GATT, COL_UCONV))
    y_ref[0, :, 0:W] = (yatt_ref[...] * _silu(g_att)).astype(BF16)

    y_conv = _silu(_layer_norm(conv_finish()) * clg_ref[0] + clb_ref[0])
    y_ref[0, :, W:2 * W] = (y_conv * _silu(g_conv)).astype(BF16)

    hf, final = lru_finish[0]()
    carry_ref[...] = final
    st_ref[0] = final
    g_lru = _dot(hc, w_in(COL_GLRU, IN_COLS))
    y_ref[0, :, 2 * W:3 * W] = ((hf + hb_ref[0]) * _silu(g_lru)).astype(BF16)


def _mix_call(x, hb, mods, P, seed, *, l, T, local, mod_row, rope=None, bias=None, kc=None, vc=None):
    B, S, _ = x.shape
    nT = S // T
    W = BRANCH_W
    tile = lambda b, i: (b, i, 0)
    batch = lambda b, i: (b, 0, 0)
    lay = lambda b, i: (l, 0, 0)
    dirn = lambda b, i: (2 * l, 0, 0)
    in_specs, args = [pl.BlockSpec((1, T, D_MODEL), tile)], [x]
    if local:
        nb = T // BLOCK
        last = S // BLOCK - 1
        left = lambda b, i: (b, jnp.maximum(i * nb - 1, 0), 0)
        right = lambda b, i: (b, jnp.minimum((i + 1) * nb, last), 0)
        in_specs += [pl.BlockSpec((1, BLOCK, D_MODEL), left), pl.BlockSpec((1, BLOCK, D_MODEL), right)]
        args += [x, x]
        cos, sin = rope
        for tab in (cos, sin):
            in_specs += [
                pl.BlockSpec((T, V7X_LANES), lambda b, i: (i, 0)),
                pl.BlockSpec((BLOCK, V7X_LANES), lambda b, i: (jnp.maximum(i * nb - 1, 0), 0)),
                pl.BlockSpec((BLOCK, V7X_LANES), lambda b, i: (jnp.minimum((i + 1) * nb, last), 0)),
            ]
            args += [tab, tab, tab]
        C = kc.shape[1]
        in_specs += [
            pl.BlockSpec((3, BLOCK, 3 * BLOCK), lambda b, i: (0, 0, 0)),
            pl.BlockSpec((1, C, KV_W), batch),
            pl.BlockSpec((1, C, V_EXT_W), batch),
        ]
        args += [bias, kc, vc]
    in_specs += [
        pl.BlockSpec((1, T, W), tile),
        pl.BlockSpec((1, V7X_SUBLANES, 3 * D_MODEL), lay),
        pl.BlockSpec((1, D_MODEL, IN_COLS), lay),
        pl.BlockSpec((1, N_HEADS, V7X_LANES), lay),
        pl.BlockSpec((1, CONV_K, W), lay),
        pl.BlockSpec((1, 1, W), lay),
        pl.BlockSpec((1, 1, W), lay),
        pl.BlockSpec((1, 1, W), lay),
        pl.BlockSpec((1, LRU_CONV_K, W), dirn),
        pl.BlockSpec((1, 1, W), dirn),
        pl.BlockSpec((1, 2 * LRU_TILES, MXU_TILE, MXU_TILE), lambda b, i, d=dirn: d(b, i) + (0,)),
        pl.BlockSpec((1, 1, 2 * W), dirn),
        pl.BlockSpec((1, 1, W), dirn),
        pl.BlockSpec((1, 1, W), batch),
    ]
    args += [hb, mods, P["w_in"], P["sink"], P["dw_w"], P["dw_b"], P["cl_g"], P["cl_b"],
             P["lru_cw"], P["lru_cb"], P["lru_wax"], P["lru_bax"], P["lru_lam"], seed]
    out_specs = [pl.BlockSpec((1, T, 3 * W), tile), pl.BlockSpec((1, 1, W), batch)]
    out_shape = [jax.ShapeDtypeStruct((B, S, 3 * W), BF16), jax.ShapeDtypeStruct((B, 1, W), F32)]
    if not local:
        out_specs += [pl.BlockSpec((1, T, KV_W), tile), pl.BlockSpec((1, T, V_EXT_W), tile)]
        out_shape += [jax.ShapeDtypeStruct((B, S, KV_W), BF16), jax.ShapeDtypeStruct((B, S, V_EXT_W), BF16)]
    halo = BLOCK if local else 0
    n_ext = T + 2 * CONV_HALO
    L = _conv_seg_len(T)
    scratch = [
        pltpu.VMEM((T + 2 * halo, D_MODEL), BF16),
        pltpu.VMEM((T + 2 * halo, KV_W), BF16),
        pltpu.VMEM((T + 2 * halo, V_EXT_W), BF16),
        pltpu.VMEM((T, W), F32),
        pltpu.VMEM((N_SLAB, n_ext, V7X_LANES), F32),
        pltpu.VMEM((N_SLAB, L + 2 * (CONV_K // 2), CONV_SEGMENTS, V7X_LANES), BF16),
        pltpu.VMEM((N_SLAB, CONV_K, CONV_SEGMENTS, V7X_LANES), BF16),
        pltpu.VMEM((N_SLAB, n_ext, V7X_LANES), F32),
        pltpu.VMEM((n_ext, W), F32),
        pltpu.VMEM((1, W), F32),
    ] + _lru_scratch(T)
    return pl.pallas_call(
        functools.partial(_mix_kernel, T=T, S=S, local=local, mod_row=mod_row),
        grid=(B, nT),
        in_specs=in_specs,
        out_specs=out_specs,
        out_shape=out_shape,
        scratch_shapes=scratch,
        compiler_params=pltpu.CompilerParams(
            dimension_semantics=("arbitrary", "arbitrary"), vmem_limit_bytes=VMEM_LIMIT_BYTES),
        name="mix_local" if local else "mix_ctx",
    )(*args)


def _merge_kernel(x_ref, y_ref, mods_ref, wgate_ref, bgate_ref, wbr_ref, wout_ref, lng_ref, lnb_ref, o_ref, *,
                  mod_row):
    x = x_ref[0]
    mod = _mod_row(mods_ref, pl.program_id(0) if mod_row is None else mod_row)
    h = _ln_mod(x, mod)
    m = None
    for n in range(N_BRANCH):
        cols = slice(n * D_MODEL, (n + 1) * D_MODEL)
        g = _sigmoid(_dot(h, wgate_ref[0, :, cols]) + bgate_ref[0, :, cols])
        p = _dot(y_ref[0, :, n * BRANCH_W:(n + 1) * BRANCH_W], wbr_ref[0, n])
        m = p * g if m is None else m + p * g
    out = _dot(m.astype(BF16), wout_ref[0])
    gate = mod[:, 2 * D_MODEL:3 * D_MODEL]
    z = DEEPNORM_ALPHA * x + gate * out
    o_ref[0] = _layer_norm(z) * lng_ref[0] + lnb_ref[0]


def _merge_call(x, y, mods, P, *, l, T, mod_row):
    B, S, _ = x.shape
    nT = S // T
    tile = lambda b, i: (b, i, 0)
    lay = lambda b, i: (l, 0, 0)
    return pl.pallas_call(
        functools.partial(_merge_kernel, mod_row=mod_row),
        grid=(B, nT),
        in_specs=[
            pl.BlockSpec((1, T, D_MODEL), tile),
            pl.BlockSpec((1, T, 3 * BRANCH_W), tile),
            pl.BlockSpec((1, V7X_SUBLANES, 3 * D_MODEL), lay),
            pl.BlockSpec((1, D_MODEL, 3 * D_MODEL), lay),
            pl.BlockSpec((1, 1, 3 * D_MODEL), lay),
            pl.BlockSpec((1, N_BRANCH, BRANCH_W, D_MODEL), lambda b, i: (l, 0, 0, 0)),
            pl.BlockSpec((1, D_MODEL, D_MODEL), lay),
            pl.BlockSpec((1, 1, D_MODEL), lay),
            pl.BlockSpec((1, 1, D_MODEL), lay),
        ],
        out_specs=pl.BlockSpec((1, T, D_MODEL), tile),
        out_shape=jax.ShapeDtypeStruct((B, S, D_MODEL), F32),
        compiler_params=pltpu.CompilerParams(
            dimension_semantics=("arbitrary", "arbitrary"), vmem_limit_bytes=VMEM_LIMIT_BYTES),
        name="merge",
    )(x, y, mods, P["w_gate"], P["b_gate"], P["w_branch"], P["w_out"], P["ln_g"], P["ln_b"])


def _rope_tables(S):
    quarter = HEAD_DIM // 4
    rows = S // GRID_W
    inv = ROPE_BASE ** (-jnp.arange(quarter, dtype=F32) / quarter)
    ang_r = jnp.arange(rows, dtype=F32)[:, None] * inv
    ang_c = jnp.arange(GRID_W, dtype=F32)[:, None] * inv

    def table(fn, signs):
        r = jnp.broadcast_to(fn(ang_r)[:, None, :], (rows, GRID_W, quarter))
        c = jnp.broadcast_to(fn(ang_c)[None, :, :], (rows, GRID_W, quarter))
        head = jnp.concatenate([signs[0] * r, signs[1] * r, signs[0] * c, signs[1] * c], axis=-1)
        return jnp.tile(head, (1, 1, V7X_LANES // HEAD_DIM)).reshape(S, V7X_LANES)

    return table(jnp.cos, (1.0, 1.0)), table(jnp.sin, (-1.0, 1.0))


def _window_bias():
    r = np.arange(BLOCK)[:, None]
    j = np.arange(3 * BLOCK)[None, :]
    band = np.abs(j - BLOCK - r) <= WINDOW
    first = band & (j >= BLOCK)
    last = band & (j < 2 * BLOCK)
    return jnp.asarray(np.where(np.stack([band, first, last]), 0.0, -np.inf), dtype=F32)


def _gate_tiles(w):
    per = MXU_TILE // w.shape[-1]
    wt = w.astype(BF16).reshape(w.shape[:2] + (LRU_TILES, per) + w.shape[-2:])
    eye = jnp.eye(per, dtype=BF16)
    dense = eye[:, None, :, None] * wt[..., :, :, None, :]
    return dense.reshape(w.shape[:2] + (LRU_TILES, MXU_TILE, MXU_TILE))


def _pack_params(w_in, attn_sink, conv_dw_w, conv_dw_b, conv_ln_g, conv_ln_b, lru_conv_w, lru_conv_b,
                 lru_w_a, lru_b_a, lru_w_x, lru_b_x, lru_lambda, w_branch, w_gate, b_gate, w_out, ln_g, ln_b):
    depth = w_in.shape[0]
    W = BRANCH_W
    rows = lambda v: v.reshape(-1, 1, v.shape[-1])
    return {
        "w_in": w_in.astype(BF16),
        "sink": jnp.broadcast_to(attn_sink[:, :, None], (depth, N_HEADS, V7X_LANES)),
        "dw_w": conv_dw_w, "dw_b": rows(conv_dw_b), "cl_g": rows(conv_ln_g), "cl_b": rows(conv_ln_b),
        "lru_cw": lru_conv_w.reshape(2 * depth, LRU_CONV_K, W),
        "lru_cb": rows(lru_conv_b),
        "lru_wax": jnp.concatenate([_gate_tiles(lru_w_a), _gate_tiles(lru_w_x)], axis=2)
                      .reshape(2 * depth, 2 * LRU_TILES, MXU_TILE, MXU_TILE),
        "lru_bax": rows(jnp.concatenate([lru_b_a, lru_b_x], axis=-1)),
        "lru_lam": rows(lru_lambda),
        "w_gate": w_gate.astype(BF16), "b_gate": rows(b_gate),
        "w_branch": w_branch.astype(BF16), "w_out": w_out.astype(BF16),
        "ln_g": rows(ln_g), "ln_b": rows(ln_b),
    }


TILE_T = 512


def kernel(x, c, ctx, c_ctx, w_ada, b_ada, w_in, attn_sink, conv_dw_w, conv_dw_b, conv_ln_g, conv_ln_b,
           lru_conv_w, lru_conv_b, lru_w_a, lru_b_a, lru_w_x, lru_b_x, lru_lambda,
           w_branch, w_gate, b_gate, w_out, ln_g, ln_b):
    B, S, D = x.shape
    C = ctx.shape[1]
    depth = w_ada.shape[0]
    assert D == D_MODEL and S % TILE_T == 0 and S >= 2 * BLOCK and B + 1 <= V7X_SUBLANES

    c_rows = jnp.concatenate([c, c_ctx[None, :], jnp.zeros((V7X_SUBLANES - B - 1, D), F32)], axis=0)
    mods = _mods_call(c_rows, w_ada, b_ada)
    P = _pack_params(w_in, attn_sink, conv_dw_w, conv_dw_b, conv_ln_g, conv_ln_b, lru_conv_w, lru_conv_b,
                     lru_w_a, lru_b_a, lru_w_x, lru_b_x, lru_lambda, w_branch, w_gate, b_gate, w_out, ln_g, ln_b)
    rope = _rope_tables(S)
    bias = _window_bias()
    zero_seed = jnp.zeros((B, 1, BRANCH_W), F32)

    for l in range(depth):
        hb_c, seed_b = _lru_bwd_call(ctx, mods, P, zero_seed, l=l, T=C, mod_row=B)
        y_c, seed_f, kc, vc = _mix_call(ctx, hb_c, mods, P, zero_seed, l=l, T=C, local=False, mod_row=B)

        hb = _lru_bwd_call(x, mods, P, seed_b, l=l, T=TILE_T, mod_row=None)[0]
        y = _mix_call(x, hb, mods, P, seed_f, l=l, T=TILE_T, local=True, mod_row=None,
                      rope=rope, bias=bias, kc=kc, vc=vc)[0]
        x = _merge_call(x, y, mods, P, l=l, T=TILE_T, mod_row=None)
        if l != depth - 1:
            ctx = _merge_call(ctx, y_c, mods, P, l=l, T=C, mod_row=B)
    return x
```

```python
import functools

import jax
import jax.numpy as jnp
import numpy as np
from jax import lax
from jax.experimental import pallas as pl
from jax.experimental.pallas import tpu as pltpu

F32 = jnp.float32
BF16 = jnp.bfloat16

D_MODEL = 1024
DEPTH = 4
GRID_W = 64
BRANCH_W = D_MODEL // 2
N_BRANCH = 3
HEAD_DIM = 64
N_HEADS = BRANCH_W // HEAD_DIM
N_KV_HEADS = N_HEADS // 4
GROUP = N_HEADS // N_KV_HEADS
KV_W = N_KV_HEADS * HEAD_DIM
V_EXT_W = N_KV_HEADS * 128
WINDOW = 128
BLOCK = 128
ROPE_BASE = 10000.0
CONV_K = 31
CONV_HALO = 16
LRU_BLOCKS = 8
LRU_CONV_K = 4
LRU_C = 8.0
DEEPNORM_ALPHA = (2 * DEPTH) ** 0.25
LN_EPS = 1e-6
ATTN_SCALE = HEAD_DIM ** -0.5

_IN_SIZES = (N_HEADS * HEAD_DIM, KV_W, KV_W, BRANCH_W, 2 * BRANCH_W, BRANCH_W, BRANCH_W, BRANCH_W)
_IN_OFF = np.cumsum((0,) + _IN_SIZES).tolist()
IN_COLS = _IN_OFF[-1]
COL_Q, COL_K, COL_V, COL_GATT, COL_UCONV, COL_GCONV, COL_XLRU, COL_GLRU = _IN_OFF[:-1]

V7X_LANES = 128
V7X_SUBLANES = 8
VMEM_LIMIT_BYTES = 56 * 1024 * 1024
N_SLAB = BRANCH_W // V7X_LANES
MXU_TILE = 256
LRU_TILES = BRANCH_W // MXU_TILE
LOG2_E = 1.4426950408889634


def _seg_len(T):
    L = (T + 2 * CONV_HALO) // V7X_SUBLANES
    assert V7X_SUBLANES * L == T + 2 * CONV_HALO and L % 8 == 4, (T, L)
    return L


def _layer_norm(x):
    mu = jnp.mean(x, axis=-1, keepdims=True)
    xc = x - mu
    var = jnp.mean(xc * xc, axis=-1, keepdims=True)
    return xc * lax.rsqrt(var + LN_EPS)


def _sigmoid(x):
    return jax.nn.sigmoid(x)


def _silu(x):
    return x * jax.nn.sigmoid(x)


def _dot(a, b):
    return jnp.dot(a, b, preferred_element_type=F32)


def _dot_nt(a, b):
    return lax.dot_general(a, b, (((1,), (1,)), ((), ())), preferred_element_type=F32)


def _ln_mod(x, mod):
    shift = mod[:, :D_MODEL]
    scale = mod[:, D_MODEL:2 * D_MODEL]
    return (_layer_norm(x) * (1.0 + scale) + shift).astype(BF16)


def _mod_row(mods_ref, row):
    if isinstance(row, int):
        return mods_ref[0, row:row + 1, :]
    return mods_ref[0, pl.ds(row, 1), :]


def _lane_slabs(x):
    return [x[:, c * V7X_LANES:(c + 1) * V7X_LANES] for c in range(x.shape[1] // V7X_LANES)]


def _mods_kernel(c_ref, w_ref, b_ref, o_ref):
    s = _silu(c_ref[...])
    o_ref[0] = _dot(s.astype(BF16), w_ref[0].astype(BF16)) + b_ref[0]


def _mods_call(c_rows, w_ada, b_ada):
    depth = w_ada.shape[0]
    ncol = 3
    return pl.pallas_call(
        _mods_kernel,
        grid=(depth, ncol),
        in_specs=[
            pl.BlockSpec((V7X_SUBLANES, D_MODEL), lambda l, j: (0, 0)),
            pl.BlockSpec((1, D_MODEL, D_MODEL), lambda l, j: (l, 0, j)),
            pl.BlockSpec((1, 1, D_MODEL), lambda l, j: (l, 0, j)),
        ],
        out_specs=pl.BlockSpec((1, V7X_SUBLANES, D_MODEL), lambda l, j: (l, 0, j)),
        out_shape=jax.ShapeDtypeStruct((depth, V7X_SUBLANES, 3 * D_MODEL), F32),
        compiler_params=pltpu.CompilerParams(
            dimension_semantics=("arbitrary", "arbitrary"), vmem_limit_bytes=VMEM_LIMIT_BYTES),
        name="adaln_mods",
    )(c_rows, w_ada, b_ada.reshape(depth, 1, 3 * D_MODEL))


def _rglru_gates(xc, wax_ref, bax_ref, lam_ref):
    xb = xc.astype(BF16)
    cols = [xb[:, t * MXU_TILE:(t + 1) * MXU_TILE] for t in range(LRU_TILES)]
    gate = lambda g: jnp.concatenate(
        [_dot(cols[t], wax_ref[0, g * LRU_TILES + t]) for t in range(LRU_TILES)], axis=1)
    r = _sigmoid(gate(0) + bax_ref[0, :, :BRANCH_W])
    ig = _sigmoid(gate(1) + bax_ref[0, :, BRANCH_W:])
    z = -lam_ref[0]
    softplus = jnp.maximum(z, 0.0) + jnp.log1p(jnp.exp(-jnp.abs(z)))
    rate = LRU_C * softplus
    t = r * rate
    a = jnp.exp2(r * (rate * (-LOG2_E)))
    u = jnp.sqrt(jnp.tanh(t) * (a * a + 1.0)) * ig * xc
    return a, u


def _scan_tasks(a, u, carry, abuf, ubuf, hbuf, *, T, reverse):
    L = _seg_len(T)
    rows = V7X_SUBLANES * L
    for c, (a_c, u_c) in enumerate(zip(_lane_slabs(a), _lane_slabs(u))):
        abuf[c, 0:T, :] = a_c
        ubuf[c, 0:T, :] = u_c
        abuf[c, T:rows, :] = jnp.ones((rows - T, V7X_LANES), F32)
        ubuf[c, T:rows, :] = jnp.zeros((rows - T, V7X_LANES), F32)
    order = range(L - 1, -1, -1) if reverse else range(L)
    seg = lambda buf, c, j: buf[c, pl.ds(j, V7X_SUBLANES, stride=L), :]
    finals = [None] * N_SLAB

    def slab_task(c, cin):
        sub = lax.broadcasted_iota(jnp.int32, (V7X_SUBLANES, V7X_LANES), 0)
        h = jnp.zeros((V7X_SUBLANES, V7X_LANES), F32)
        p = jnp.ones((V7X_SUBLANES, V7X_LANES), F32)
        for j in order:
            a_j = seg(abuf, c, j)
            h = a_j * h + seg(ubuf, c, j)
            p = a_j * p
        for d in (1, 2, 4):
            sh = V7X_SUBLANES - d if reverse else d
            m = (sub < V7X_SUBLANES - d) if reverse else (sub >= d)
            h = jnp.where(m, p * pltpu.roll(h, sh, axis=0) + h, h)
            p = jnp.where(m, p * pltpu.roll(p, sh, axis=0), p)
        end = h + p * cin
        if reverse:
            h = jnp.where(sub == V7X_SUBLANES - 1, cin, pltpu.roll(end, V7X_SUBLANES - 1, axis=0))
            finals[c] = end[0:1, :]
        else:
            h = jnp.where(sub == 0, cin, pltpu.roll(end, 1, axis=0))
            finals[c] = end[V7X_SUBLANES - 1:V7X_SUBLANES, :]
        for j in order:
            h = seg(abuf, c, j) * h + seg(ubuf, c, j)
            hbuf[c, pl.ds(j, V7X_SUBLANES, stride=L), :] = h

    tasks = [functools.partial(slab_task, c, cin) for c, cin in enumerate(_lane_slabs(carry))]

    def finish():
        h_all = jnp.concatenate([hbuf[c, 0:T, :] for c in range(N_SLAB)], axis=1)
        return h_all, jnp.concatenate(finals, axis=1)

    return tasks, finish


def _scan_segments(a, u, carry, abuf, ubuf, hbuf, *, T, reverse):
    tasks, finish = _scan_tasks(a, u, carry, abuf, ubuf, hbuf, T=T, reverse=reverse)
    for task in tasks:
        task()
    return finish()


CONV_SEGMENTS = 2 * V7X_SUBLANES


def _conv_seg_len(T):
    L = (T + 2 * CONV_HALO) // CONV_SEGMENTS
    assert CONV_SEGMENTS * L == T + 2 * CONV_HALO and L % 8 != 0 and L >= CONV_K // 2, (T, L)
    return L


def _dwconv_tasks(ybuf, yext, wext, obuf, w_ref, b_ref, *, T, K, chunks):
    L = _conv_seg_len(T)
    half = K // 2
    hi = V7X_SUBLANES * L

    def stage(c):
        lanes = slice(c * V7X_LANES, (c + 1) * V7X_LANES)
        for k in range(K):
            wext[c, k] = jnp.broadcast_to(w_ref[0, k:k + 1, lanes], (CONV_SEGMENTS, V7X_LANES)).astype(BF16)
        for j in range(L):
            v = jnp.concatenate([ybuf[c, pl.ds(j, V7X_SUBLANES, stride=L), :],
                                 ybuf[c, pl.ds(hi + j, V7X_SUBLANES, stride=L), :]], axis=0)
            yext[c, half + j] = v.astype(BF16)
            if j >= L - half:
                yext[c, half + j - L] = pltpu.roll(v, 1, axis=0).astype(BF16)
            if j < half:
                yext[c, half + j + L] = pltpu.roll(v, CONV_SEGMENTS - 1, axis=0).astype(BF16)

    def taps(c, j0, j1):
        lanes = slice(c * V7X_LANES, (c + 1) * V7X_LANES)
        bc = jnp.broadcast_to(b_ref[0, :, lanes], (CONV_SEGMENTS, V7X_LANES))
        for j in range(j0, j1):
            acc = jnp.zeros((CONV_SEGMENTS, V7X_LANES), F32)
            for k in range(K):
                acc = acc + yext[c, j + k].astype(F32) * wext[c, k].astype(F32)
            acc = acc + bc
            obuf[c, pl.ds(j, V7X_SUBLANES, stride=L), :] = acc[0:V7X_SUBLANES]
            obuf[c, pl.ds(hi + j, V7X_SUBLANES, stride=L), :] = acc[V7X_SUBLANES:]

    step = -(-L // chunks)
    tasks = [functools.partial(stage, c) for c in range(N_SLAB)]
    for c in range(N_SLAB):
        tasks += [functools.partial(taps, c, j0, min(j0 + step, L)) for j0 in range(0, L, step)]

    def finish():
        return jnp.concatenate([obuf[c, 0:T, :] for c in range(N_SLAB)], axis=1)

    return tasks, finish


def _lru_scratch(T):
    rows = V7X_SUBLANES * _seg_len(T)
    return [pltpu.VMEM((N_SLAB, rows, V7X_LANES), F32) for _ in range(3)]


def _lru_bwd_kernel(*refs, T, halo, mod_row):
    if halo:
        (x_ref, xh_ref, mods_ref, wlo_ref, whi_ref, cw_ref, cb_ref, wax_ref, bax_ref, lam_ref, seed_ref,
         hb_ref, st_ref, buf_ref, carry_ref, abuf, ubuf, hbuf) = refs
    else:
        (x_ref, mods_ref, wlo_ref, whi_ref, cw_ref, cb_ref, wax_ref, bax_ref, lam_ref, seed_ref,
         hb_ref, st_ref, buf_ref, carry_ref, abuf, ubuf, hbuf) = refs
    i = pl.program_id(1)
    mod = _mod_row(mods_ref, pl.program_id(0) if mod_row is None else mod_row)

    @pl.when(i == 0)
    def _():
        carry_ref[...] = seed_ref[0]

    def x_lru(xv):
        h = _ln_mod(xv, mod)
        return jnp.concatenate([_dot(h, wlo_ref[0]), _dot(h, whi_ref[0])], axis=1)

    xl = x_lru(x_ref[0])
    buf_ref[0:T, :] = xl
    if halo:
        buf_ref[T:T + V7X_SUBLANES, :] = jnp.where(i == 0, 0.0, x_lru(xh_ref[0]))
    else:
        buf_ref[T:T + V7X_SUBLANES, :] = jnp.zeros((V7X_SUBLANES, BRANCH_W), F32)
    xc = cb_ref[0] + cw_ref[0, 0:1, :] * xl
    for k in range(1, LRU_CONV_K):
        xc = xc + cw_ref[0, k:k + 1, :] * buf_ref[k:k + T, :]
    a, u = _rglru_gates(xc, wax_ref, bax_ref, lam_ref)
    h, final = _scan_segments(a, u, carry_ref[...], abuf, ubuf, hbuf, T=T, reverse=True)
    hb_ref[0] = h
    carry_ref[...] = final
    st_ref[0] = final


def _lru_bwd_call(x, mods, P, seed, *, l, T, mod_row):
    B, S, _ = x.shape
    nT = S // T
    halo = nT > 1
    W = BRANCH_W
    d = 2 * l + 1
    rev = lambda b, i: (b, nT - 1 - i, 0)
    lay = lambda b, i: (l, 0, 0)
    dirn = lambda b, i: (d, 0, 0)
    in_specs = [pl.BlockSpec((1, T, D_MODEL), rev)]
    args = [x]
    if halo:
        hb = T // V7X_SUBLANES
        last = S // V7X_SUBLANES - 1
        in_specs.append(pl.BlockSpec((1, V7X_SUBLANES, D_MODEL),
                                     lambda b, i: (b, jnp.minimum((nT - i) * hb, last), 0)))
        args.append(x)
    half = W // 2
    in_specs += [
        pl.BlockSpec((1, V7X_SUBLANES, 3 * D_MODEL), lay),
        pl.BlockSpec((1, D_MODEL, half), lambda b, i: (l, 0, COL_XLRU // half)),
        pl.BlockSpec((1, D_MODEL, half), lambda b, i: (l, 0, COL_XLRU // half + 1)),
        pl.BlockSpec((1, LRU_CONV_K, W), dirn),
        pl.BlockSpec((1, 1, W), dirn),
        pl.BlockSpec((1, 2 * LRU_TILES, MXU_TILE, MXU_TILE), lambda b, i, d=dirn: d(b, i) + (0,)),
        pl.BlockSpec((1, 1, 2 * W), dirn),
        pl.BlockSpec((1, 1, W), dirn),
        pl.BlockSpec((1, 1, W), lambda b, i: (b, 0, 0)),
    ]
    args += [mods, P["w_in"], P["w_in"], P["lru_cw"], P["lru_cb"], P["lru_wax"], P["lru_bax"], P["lru_lam"], seed]
    return pl.pallas_call(
        functools.partial(_lru_bwd_kernel, T=T, halo=halo, mod_row=mod_row),
        grid=(B, nT),
        in_specs=in_specs,
        out_specs=[pl.BlockSpec((1, T, W), rev), pl.BlockSpec((1, 1, W), lambda b, i: (b, 0, 0))],
        out_shape=[jax.ShapeDtypeStruct((B, S, W), F32), jax.ShapeDtypeStruct((B, 1, W), F32)],
        scratch_shapes=[pltpu.VMEM((T + V7X_SUBLANES, W), F32), pltpu.VMEM((1, W), F32)] + _lru_scratch(T),
        compiler_params=pltpu.CompilerParams(
            dimension_semantics=("arbitrary", "arbitrary"), vmem_limit_bytes=VMEM_LIMIT_BYTES),
        name="lru_bwd",
    )(*args)


def _rope(x, cos, sin_signed, even):
    outs = []
    for xs in _lane_slabs(x):
        up = pltpu.roll(xs, V7X_LANES - HEAD_DIM // 4, axis=1)
        dn = pltpu.roll(xs, HEAD_DIM // 4, axis=1)
        outs.append(xs * cos + jnp.where(even, up, dn) * sin_signed)
    return outs[0] if len(outs) == 1 else jnp.concatenate(outs, axis=1)


def _mix_kernel(*refs, T, S, local, mod_row):
    W = BRANCH_W
    if local:
        (x_ref, xl_ref, xr_ref, cos_ref, cosl_ref, cosr_ref, sin_ref, sinl_ref, sinr_ref, bias_ref,
         kc_ref, vc_ref, hb_ref, mods_ref, win_ref, sink_ref,
         dww_ref, dwb_ref, clg_ref, clb_ref,
         lcw_ref, lcb_ref, wax_ref, bax_ref, lam_ref, seed_ref,
         y_ref, st_ref,
         hext_ref, kext_ref, vext_ref, yatt_ref, cbuf, yext, wext, obuf, lbuf_ref, carry_ref,
         abuf, ubuf, hbuf) = refs
        halo = BLOCK
    else:
        (x_ref, hb_ref, mods_ref, win_ref, sink_ref,
         dww_ref, dwb_ref, clg_ref, clb_ref,
         lcw_ref, lcb_ref, wax_ref, bax_ref, lam_ref, seed_ref,
         y_ref, st_ref, ko_ref, vo_ref,
         hext_ref, kext_ref, vext_ref, yatt_ref, cbuf, yext, wext, obuf, lbuf_ref, carry_ref,
         abuf, ubuf, hbuf) = refs
        halo = 0
    i = pl.program_id(1)
    mod = _mod_row(mods_ref, pl.program_id(0) if mod_row is None else mod_row)
    w_in = lambda c0, c1: win_ref[0, :, c0:c1]

    @pl.when(i == 0)
    def _():
        carry_ref[...] = seed_ref[0]

    hext_ref[halo:halo + T, :] = _ln_mod(x_ref[0], mod)
    if local:
        hext_ref[0:halo, :] = _ln_mod(xl_ref[0], mod)
        hext_ref[halo + T:halo + T + halo, :] = _ln_mod(xr_ref[0], mod)
    hc = hext_ref[halo:halo + T, :]

    state = {}
    if local:
        lane = lax.broadcasted_iota(jnp.int32, (1, V7X_LANES), 1)
        even = (lane & (HEAD_DIM // 4)) == 0

    def with_ones(v):
        ones = jnp.ones((v.shape[0], HEAD_DIM), F32)
        return jnp.concatenate([v[:, :HEAD_DIM], ones, v[:, HEAD_DIM:], ones], axis=1).astype(BF16)

    def proj_kv():
        w_kv = w_in(COL_K, COL_GATT)
        if local:
            parts = ((0, halo, cosl_ref, sinl_ref), (halo, T, cos_ref, sin_ref),
                     (halo + T, halo, cosr_ref, sinr_ref))
            for r0, n, c_ref, s_ref in parts:
                kv = _dot(hext_ref[r0:r0 + n, :], w_kv)
                kext_ref[r0:r0 + n, :] = _rope(kv[:, :KV_W], c_ref[...], s_ref[...], even).astype(BF16)
                vext_ref[r0:r0 + n, :] = with_ones(kv[:, KV_W:])
        else:
            kv = _dot(hc, w_kv)
            kext_ref[...] = kv[:, :KV_W].astype(BF16)
            vext_ref[...] = with_ones(kv[:, KV_W:])
            ko_ref[0] = kext_ref[...]
            vo_ref[0] = vext_ref[...]

    def proj_q():
        q = _dot(hc, w_in(COL_Q, COL_K))
        if local:
            q = _rope(q, cos_ref[...], sin_ref[...], even)
        state["q"] = q * (ATTN_SCALE * LOG2_E)

    def attention(jb, hk):
        q = state["q"]
        r0 = jb * BLOCK
        heads = [GROUP * hk + g for g in range(GROUP)]
        q4 = jnp.concatenate(
            [q[r0:r0 + BLOCK, h * HEAD_DIM:(h + 1) * HEAD_DIM] for h in heads], axis=0).astype(BF16)
        sink = LOG2_E * jnp.concatenate(
            [jnp.broadcast_to(sink_ref[0, h:h + 1, 0:1], (BLOCK, 1)) for h in heads], axis=0)
        ks = slice(hk * HEAD_DIM, (hk + 1) * HEAD_DIM)
        vs = slice(hk * V7X_LANES, (hk + 1) * V7X_LANES)
        if local:
            k_c, v_c = kc_ref[0, :, ks], vc_ref[0, :, vs]
        else:
            k_c, v_c = kext_ref[:, ks], vext_ref[:, vs]
        s_ctx = _dot_nt(q4, k_c)
        m = jnp.maximum(jnp.max(s_ctx, axis=-1, keepdims=True), sink)
        if local:
            pos0 = i * T + r0
            bidx = jnp.where(pos0 == 0, 1, jnp.where(pos0 == S - BLOCK, 2, 0))
            bias = bias_ref[bidx]
            k_w = kext_ref[r0:r0 + 3 * BLOCK, ks]
            v_w = vext_ref[r0:r0 + 3 * BLOCK, vs]
            s_loc = _dot_nt(q4, k_w)
            s_loc = jnp.concatenate(
                [s_loc[g * BLOCK:(g + 1) * BLOCK] + bias for g in range(GROUP)], axis=0)
            m = jnp.maximum(m, jnp.max(s_loc, axis=-1, keepdims=True))
        o = _dot(jnp.exp2(s_ctx - m).astype(BF16), v_c)
        if local:
            o = o + _dot(jnp.exp2(s_loc - m).astype(BF16), v_w)
        den = pltpu.roll(o, HEAD_DIM, axis=1) + jnp.exp2(sink - m)
        o = o / den
        for g, h in enumerate(heads):
            yatt_ref[r0:r0 + BLOCK, h * HEAD_DIM:(h + 1) * HEAD_DIM] = o[g * BLOCK:(g + 1) * BLOCK, :HEAD_DIM]

    n_ext = T + 2 * CONV_HALO
    zeros = jnp.zeros((CONV_HALO, W), F32)
    h_wide = hext_ref[halo - CONV_HALO:halo + T + CONV_HALO, :] if local else hc
    if local:
        pos = i * T - CONV_HALO + lax.broadcasted_iota(jnp.int32, (n_ext, 1), 0)
        inside = (pos >= 0) & (pos < S)
    conv_tasks, conv_finish = _dwconv_tasks(cbuf, yext, wext, obuf, dww_ref, dwb_ref, T=T, K=CONV_K, chunks=2)
    lru_tasks, lru_finish = [], []

    def proj_glu():
        ag = _dot(h_wide, w_in(COL_UCONV, COL_GCONV))
        glu = ag[:, 0:W] * _sigmoid(ag[:, W:2 * W])
        if local:
            glu = jnp.where(inside, glu, 0.0)
            parts = (glu[CONV_HALO:CONV_HALO + T], glu[CONV_HALO + T:n_ext], glu[0:CONV_HALO])
        else:
            parts = (glu, zeros, zeros)
        for r0, part in zip((0, T, T + CONV_HALO), parts):
            for c, slab in enumerate(_lane_slabs(part)):
                cbuf[c, r0:r0 + part.shape[0], :] = slab
        side.append(conv_tasks)

    def lru_front():
        xc = jnp.broadcast_to(lcb_ref[0], (T, W))
        off = CONV_HALO - (LRU_CONV_K - 1)
        for k in range(LRU_CONV_K):
            xc = xc + lcw_ref[0, k:k + 1, :] * lbuf_ref[off + k:off + k + T, :]
        a, u = _rglru_gates(xc, wax_ref, bax_ref, lam_ref)
        tasks, finish = _scan_tasks(a, u, carry_ref[...], abuf, ubuf, hbuf, T=T, reverse=False)
        lru_tasks.extend(tasks)
        lru_finish.append(finish)

    def proj_gate_lru():
        gx = _dot(h_wide, w_in(COL_GCONV, COL_GLRU))
        if local:
            lbuf_ref[...] = jnp.where(inside, gx[:, W:2 * W], 0.0)
            state["g_conv"] = gx[CONV_HALO:CONV_HALO + T, 0:W]
        else:
            lbuf_ref[0:CONV_HALO, :] = zeros
            lbuf_ref[CONV_HALO + T:n_ext, :] = zeros
            lbuf_ref[CONV_HALO:CONV_HALO + T, :] = gx[:, W:2 * W]
            state["g_conv"] = gx[:, 0:W]
        lru_tasks.append(lru_front)
        side.append(lru_tasks)

    side = []
    for proj in (proj_kv, proj_q, proj_glu, proj_gate_lru):
        proj()
    blocks = [(jb, hk) for jb in range(T // BLOCK) for hk in range(N_KV_HEADS)]
    for n, (jb, hk) in enumerate(blocks):
        attention(jb, hk)
        left = len(blocks) - n
        for queue in side:
            for _ in range(-(-len(queue) // left)):
                queue.pop(0)()
    assert not conv_tasks and not lru_tasks
    g_conv = state["g_conv"]

    g_att = _dot(hc, w_in(COL_GATT, COL_UCONV))
    y_ref[0, :, 0:W] = (yatt_ref[...] * _silu(g_att)).astype(BF16)

    y_conv = _silu(_layer_norm(conv_finish()) * clg_ref[0] + clb_ref[0])
    y_ref[0, :, W:2 * W] = (y_conv * _silu(g_conv)).astype(BF16)

    hf, final = lru_finish[0]()
    carry_ref[...] = final
    st_ref[0] = final
    g_lru = _dot(hc, w_in(COL_GLRU, IN_COLS))
    y_ref[0, :, 2 * W:3 * W] = ((hf + hb_ref[0]) * _silu(g_lru)).astype(BF16)


def _mix_call(x, hb, mods, P, seed, *, l, T, local, mod_row, rope=None, bias=None, kc=None, vc=None):
    B, S, _ = x.shape
    nT = S // T
    W = BRANCH_W
    tile = lambda b, i: (b, i, 0)
    batch = lambda b, i: (b, 0, 0)
    lay = lambda b, i: (l, 0, 0)
    dirn = lambda b, i: (2 * l, 0, 0)
    in_specs, args = [pl.BlockSpec((1, T, D_MODEL), tile)], [x]
    if local:
        nb = T // BLOCK
        last = S // BLOCK - 1
        left = lambda b, i: (b, jnp.maximum(i * nb - 1, 0), 0)
        right = lambda b, i: (b, jnp.minimum((i + 1) * nb, last), 0)
        in_specs += [pl.BlockSpec((1, BLOCK, D_MODEL), left), pl.BlockSpec((1, BLOCK, D_MODEL), right)]
        args += [x, x]
        cos, sin = rope
        for tab in (cos, sin):
            in_specs += [
                pl.BlockSpec((T, V7X_LANES), lambda b, i: (i, 0)),
                pl.BlockSpec((BLOCK, V7X_LANES), lambda b, i: (jnp.maximum(i * nb - 1, 0), 0)),
                pl.BlockSpec((BLOCK, V7X_LANES), lambda b, i: (jnp.minimum((i + 1) * nb, last), 0)),
            ]
            args += [tab, tab, tab]
        C = kc.shape[1]
        in_specs += [
            pl.BlockSpec((3, BLOCK, 3 * BLOCK), lambda b, i: (0, 0, 0)),
            pl.BlockSpec((1, C, KV_W), batch),
            pl.BlockSpec((1, C, V_EXT_W), batch),
        ]
        args += [bias, kc, vc]
    in_specs += [
        pl.BlockSpec((1, T, W), tile),
        pl.BlockSpec((1, V7X_SUBLANES, 3 * D_MODEL), lay),
        pl.BlockSpec((1, D_MODEL, IN_COLS), lay),
        pl.BlockSpec((1, N_HEADS, V7X_LANES), lay),
        pl.BlockSpec((1, CONV_K, W), lay),
        pl.BlockSpec((1, 1, W), lay),
        pl.BlockSpec((1, 1, W), lay),
        pl.BlockSpec((1, 1, W), lay),
        pl.BlockSpec((1, LRU_CONV_K, W), dirn),
        pl.BlockSpec((1, 1, W), dirn),
        pl.BlockSpec((1, 2 * LRU_TILES, MXU_TILE, MXU_TILE), lambda b, i, d=dirn: d(b, i) + (0,)),
        pl.BlockSpec((1, 1, 2 * W), dirn),
        pl.BlockSpec((1, 1, W), dirn),
        pl.BlockSpec((1, 1, W), batch),
    ]
    args += [hb, mods, P["w_in"], P["sink"], P["dw_w"], P["dw_b"], P["cl_g"], P["cl_b"],
             P["lru_cw"], P["lru_cb"], P["lru_wax"], P["lru_bax"], P["lru_lam"], seed]
    out_specs = [pl.BlockSpec((1, T, 3 * W), tile), pl.BlockSpec((1, 1, W), batch)]
    out_shape = [jax.ShapeDtypeStruct((B, S, 3 * W), BF16), jax.ShapeDtypeStruct((B, 1, W), F32)]
    if not local:
        out_specs += [pl.BlockSpec((1, T, KV_W), tile), pl.BlockSpec((1, T, V_EXT_W), tile)]
        out_shape += [jax.ShapeDtypeStruct((B, S, KV_W), BF16), jax.ShapeDtypeStruct((B, S, V_EXT_W), BF16)]
    halo = BLOCK if local else 0
    n_ext = T + 2 * CONV_HALO
    L = _conv_seg_len(T)
    scratch = [
        pltpu.VMEM((T + 2 * halo, D_MODEL), BF16),
        pltpu.VMEM((T + 2 * halo, KV_W), BF16),
        pltpu.VMEM((T + 2 * halo, V_EXT_W), BF16),
        pltpu.VMEM((T, W), F32),
        pltpu.VMEM((N_SLAB, n_ext, V7X_LANES), F32),
        pltpu.VMEM((N_SLAB, L + 2 * (CONV_K // 2), CONV_SEGMENTS, V7X_LANES), BF16),
        pltpu.VMEM((N_SLAB, CONV_K, CONV_SEGMENTS, V7X_LANES), BF16),
        pltpu.VMEM((N_SLAB, n_ext, V7X_LANES), F32),
        pltpu.VMEM((n_ext, W), F32),
        pltpu.VMEM((1, W), F32),
    ] + _lru_scratch(T)
    return pl.pallas_call(
        functools.partial(_mix_kernel, T=T, S=S, local=local, mod_row=mod_row),
        grid=(B, nT),
        in_specs=in_specs,
        out_specs=out_specs,
        out_shape=out_shape,
        scratch_shapes=scratch,
        compiler_params=pltpu.CompilerParams(
            dimension_semantics=("arbitrary", "arbitrary"), vmem_limit_bytes=VMEM_LIMIT_BYTES),
        name="mix_local" if local else "mix_ctx",
    )(*args)


def _merge_kernel(x_ref, y_ref, mods_ref, wgate_ref, bgate_ref, wbr_ref, wout_ref, lng_ref, lnb_ref, o_ref, *,
                  mod_row):
    x = x_ref[0]
    mod = _mod_row(mods_ref, pl.program_id(0) if mod_row is None else mod_row)
    h = _ln_mod(x, mod)
    m = None
    for n in range(N_BRANCH):
        cols = slice(n * D_MODEL, (n + 1) * D_MODEL)
        g = _sigmoid(_dot(h, wgate_ref[0, :, cols]) + bgate_ref[0, :, cols])
        p = _dot(y_ref[0, :, n * BRANCH_W:(n + 1) * BRANCH_W], wbr_ref[0, n])
        m = p * g if m is None else m + p * g
    out = _dot(m.astype(BF16), wout_ref[0])
    gate = mod[:, 2 * D_MODEL:3 * D_MODEL]
    z = DEEPNORM_ALPHA * x + gate * out
    o_ref[0] = _layer_norm(z) * lng_ref[0] + lnb_ref[0]


def _merge_call(x, y, mods, P, *, l, T, mod_row):
    B, S, _ = x.shape
    nT = S // T
    tile = lambda b, i: (b, i, 0)
    lay = lambda b, i: (l, 0, 0)
    return pl.pallas_call(
        functools.partial(_merge_kernel, mod_row=mod_row),
        grid=(B, nT),
        in_specs=[
            pl.BlockSpec((1, T, D_MODEL), tile),
            pl.BlockSpec((1, T, 3 * BRANCH_W), tile),
            pl.BlockSpec((1, V7X_SUBLANES, 3 * D_MODEL), lay),
            pl.BlockSpec((1, D_MODEL, 3 * D_MODEL), lay),
            pl.BlockSpec((1, 1, 3 * D_MODEL), lay),
            pl.BlockSpec((1, N_BRANCH, BRANCH_W, D_MODEL), lambda b, i: (l, 0, 0, 0)),
            pl.BlockSpec((1, D_MODEL, D_MODEL), lay),
            pl.BlockSpec((1, 1, D_MODEL), lay),
            pl.BlockSpec((1, 1, D_MODEL), lay),
        ],
        out_specs=pl.BlockSpec((1, T, D_MODEL), tile),
        out_shape=jax.ShapeDtypeStruct((B, S, D_MODEL), F32),
        compiler_params=pltpu.CompilerParams(
            dimension_semantics=("arbitrary", "arbitrary"), vmem_limit_bytes=VMEM_LIMIT_BYTES),
        name="merge",
    )(x, y, mods, P["w_gate"], P["b_gate"], P["w_branch"], P["w_out"], P["ln_g"], P["ln_b"])


def _rope_tables(S):
    quarter = HEAD_DIM // 4
    rows = S // GRID_W
    inv = ROPE_BASE ** (-jnp.arange(quarter, dtype=F32) / quarter)
    ang_r = jnp.arange(rows, dtype=F32)[:, None] * inv
    ang_c = jnp.arange(GRID_W, dtype=F32)[:, None] * inv

    def table(fn, signs):
        r = jnp.broadcast_to(fn(ang_r)[:, None, :], (rows, GRID_W, quarter))
        c = jnp.broadcast_to(fn(ang_c)[None, :, :], (rows, GRID_W, quarter))
        head = [signs[0] * r, signs[1] * r, signs[0] * c, signs[1] * c]
        return jnp.concatenate(head * (V7X_LANES // HEAD_DIM), axis=-1).reshape(S, V7X_LANES)

    return table(jnp.cos, (1.0, 1.0)), table(jnp.sin, (-1.0, 1.0))


def _window_bias():
    r = np.arange(BLOCK)[:, None]
    j = np.arange(3 * BLOCK)[None, :]
    band = np.abs(j - BLOCK - r) <= WINDOW
    first = band & (j >= BLOCK)
    last = band & (j < 2 * BLOCK)
    return jnp.asarray(np.where(np.stack([band, first, last]), 0.0, -np.inf), dtype=F32)


def _gate_tiles(w):
    blk = w.shape[-1]
    per = MXU_TILE // blk
    wt = w.astype(BF16).reshape(w.shape[:2] + (LRU_TILES, per, blk, blk))
    lead = ((0, 0),) * 3
    return sum(jnp.pad(wt[:, :, :, n], lead + ((n * blk, MXU_TILE - (n + 1) * blk),) * 2) for n in range(per))


def _pack_params(w_in, attn_sink, conv_dw_w, conv_dw_b, conv_ln_g, conv_ln_b, lru_conv_w, lru_conv_b,
                 lru_w_a, lru_b_a, lru_w_x, lru_b_x, lru_lambda, w_branch, w_gate, b_gate, w_out, ln_g, ln_b):
    depth = w_in.shape[0]
    W = BRANCH_W
    rows = lambda v: v.reshape(-1, 1, v.shape[-1])
    return {
        "w_in": w_in.astype(BF16),
        "sink": jnp.broadcast_to(attn_sink[:, :, None], (depth, N_HEADS, V7X_LANES)),
        "dw_w": conv_dw_w, "dw_b": rows(conv_dw_b), "cl_g": rows(conv_ln_g), "cl_b": rows(conv_ln_b),
        "lru_cw": lru_conv_w.reshape(2 * depth, LRU_CONV_K, W),
        "lru_cb": rows(lru_conv_b),
        "lru_wax": jnp.concatenate([_gate_tiles(lru_w_a), _gate_tiles(lru_w_x)], axis=2)
                      .reshape(2 * depth, 2 * LRU_TILES, MXU_TILE, MXU_TILE),
        "lru_bax": rows(jnp.concatenate([lru_b_a, lru_b_x], axis=-1)),
        "lru_lam": rows(lru_lambda),
        "w_gate": w_gate.astype(BF16), "b_gate": rows(b_gate),
        "w_branch": w_branch.astype(BF16), "w_out": w_out.astype(BF16),
        "ln_g": rows(ln_g), "ln_b": rows(ln_b),
    }


TILE_T = 512


def kernel(x, c, ctx, c_ctx, w_ada, b_ada, w_in, attn_sink, conv_dw_w, conv_dw_b, conv_ln_g, conv_ln_b,
           lru_conv_w, lru_conv_b, lru_w_a, lru_b_a, lru_w_x, lru_b_x, lru_lambda,
           w_branch, w_gate, b_gate, w_out, ln_g, ln_b):
    B, S, D = x.shape
    C = ctx.shape[1]
    depth = w_ada.shape[0]
    assert D == D_MODEL and S % TILE_T == 0 and S >= 2 * BLOCK and B + 1 <= V7X_SUBLANES

    c_rows = jnp.concatenate([c, c_ctx[None, :], jnp.zeros((V7X_SUBLANES - B - 1, D), F32)], axis=0)
    mods = _mods_call(c_rows, w_ada, b_ada)
    P = _pack_params(w_in, attn_sink, conv_dw_w, conv_dw_b, conv_ln_g, conv_ln_b, lru_conv_w, lru_conv_b,
                     lru_w_a, lru_b_a, lru_w_x, lru_b_x, lru_lambda, w_branch, w_gate, b_gate, w_out, ln_g, ln_b)
    rope = _rope_tables(S)
    bias = _window_bias()
    zero_seed = jnp.zeros((B, 1, BRANCH_W), F32)

    for l in range(depth):
        hb_c, seed_b = _lru_bwd_call(ctx, mods, P, zero_seed, l=l, T=C, mod_row=B)
        y_c, seed_f, kc, vc = _mix_call(ctx, hb_c, mods, P, zero_seed, l=l, T=C, local=False, mod_row=B)

        hb = _lru_bwd_call(x, mods, P, seed_b, l=l, T=TILE_T, mod_row=None)[0]
        y = _mix_call(x, hb, mods, P, seed_f, l=l, T=TILE_T, local=True, mod_row=None,
                      rope=rope, bias=bias, kc=kc, vc=vc)[0]
        x = _merge_call(x, y, mods, P, l=l, T=TILE_T, mod_row=None)
        if l != depth - 1:
            ctx = _merge_call(ctx, y_c, mods, P, l=l, T=C, mod_row=B)
    return x
```

```python
import functools

import jax
import jax.numpy as jnp
import numpy as np
from jax import lax
from jax.experimental import pallas as pl
from jax.experimental.pallas import tpu as pltpu

F32 = jnp.float32
BF16 = jnp.bfloat16

D_MODEL = 1024
DEPTH = 4
GRID_W = 64
BRANCH_W = D_MODEL // 2
N_BRANCH = 3
HEAD_DIM = 64
N_HEADS = BRANCH_W // HEAD_DIM
N_KV_HEADS = N_HEADS // 4
GROUP = N_HEADS // N_KV_HEADS
KV_W = N_KV_HEADS * HEAD_DIM
V_EXT_W = N_KV_HEADS * 128
WINDOW = 128
BLOCK = 128
ROPE_BASE = 10000.0
CONV_K = 31
CONV_HALO = 16
LRU_BLOCKS = 8
LRU_CONV_K = 4
LRU_C = 8.0
DEEPNORM_ALPHA = (2 * DEPTH) ** 0.25
LN_EPS = 1e-6
ATTN_SCALE = HEAD_DIM ** -0.5

_IN_SIZES = (N_HEADS * HEAD_DIM, KV_W, KV_W, BRANCH_W, 2 * BRANCH_W, BRANCH_W, BRANCH_W, BRANCH_W)
_IN_OFF = np.cumsum((0,) + _IN_SIZES).tolist()
IN_COLS = _IN_OFF[-1]
COL_Q, COL_K, COL_V, COL_GATT, COL_UCONV, COL_GCONV, COL_XLRU, COL_GLRU = _IN_OFF[:-1]

V7X_LANES = 128
V7X_SUBLANES = 8
VMEM_LIMIT_BYTES = 56 * 1024 * 1024
N_SLAB = BRANCH_W // V7X_LANES
MXU_TILE = 256
LRU_TILES = BRANCH_W // MXU_TILE
LOG2_E = 1.4426950408889634


def _seg_len(T):
    L = (T + 2 * CONV_HALO) // V7X_SUBLANES
    assert V7X_SUBLANES * L == T + 2 * CONV_HALO and L % 8 == 4, (T, L)
    return L


def _layer_norm(x):
    mu = jnp.mean(x, axis=-1, keepdims=True)
    xc = x - mu
    var = jnp.mean(xc * xc, axis=-1, keepdims=True)
    return xc * lax.rsqrt(var + LN_EPS)


def _sigmoid(x):
    return jax.nn.sigmoid(x)


def _silu(x):
    return x * jax.nn.sigmoid(x)


def _dot(a, b):
    return jnp.dot(a, b, preferred_element_type=F32)


def _dot_nt(a, b):
    return lax.dot_general(a, b, (((1,), (1,)), ((), ())), preferred_element_type=F32)


def _ln_mod(x, mod):
    shift = mod[:, :D_MODEL]
    scale = mod[:, D_MODEL:2 * D_MODEL]
    return (_layer_norm(x) * (1.0 + scale) + shift).astype(BF16)


def _mod_row(mods_ref, row):
    if isinstance(row, int):
        return mods_ref[0, row:row + 1, :]
    return mods_ref[0, pl.ds(row, 1), :]


def _lane_slabs(x):
    return [x[:, c * V7X_LANES:(c + 1) * V7X_LANES] for c in range(x.shape[1] // V7X_LANES)]


def _mods_kernel(c_ref, w_ref, b_ref, o_ref):
    s = _silu(c_ref[...])
    o_ref[0] = _dot(s.astype(BF16), w_ref[0].astype(BF16)) + b_ref[0]


def _mods_call(c_rows, w_ada, b_ada):
    depth = w_ada.shape[0]
    ncol = 3
    return pl.pallas_call(
        _mods_kernel,
        grid=(depth, ncol),
        in_specs=[
            pl.BlockSpec((V7X_SUBLANES, D_MODEL), lambda l, j: (0, 0)),
            pl.BlockSpec((1, D_MODEL, D_MODEL), lambda l, j: (l, 0, j)),
            pl.BlockSpec((1, 1, D_MODEL), lambda l, j: (l, 0, j)),
        ],
        out_specs=pl.BlockSpec((1, V7X_SUBLANES, D_MODEL), lambda l, j: (l, 0, j)),
        out_shape=jax.ShapeDtypeStruct((depth, V7X_SUBLANES, 3 * D_MODEL), F32),
        compiler_params=pltpu.CompilerParams(
            dimension_semantics=("arbitrary", "arbitrary"), vmem_limit_bytes=VMEM_LIMIT_BYTES),
        name="adaln_mods",
    )(c_rows, w_ada, b_ada.reshape(depth, 1, 3 * D_MODEL))


def _rglru_gates(xc, wax_ref, bax_ref, lam_ref):
    xb = xc.astype(BF16)
    cols = [xb[:, t * MXU_TILE:(t + 1) * MXU_TILE] for t in range(LRU_TILES)]
    gate = lambda g: jnp.concatenate(
        [_dot(cols[t], wax_ref[0, g * LRU_TILES + t]) for t in range(LRU_TILES)], axis=1)
    r = _sigmoid(gate(0) + bax_ref[0, :, :BRANCH_W])
    ig = _sigmoid(gate(1) + bax_ref[0, :, BRANCH_W:])
    z = -lam_ref[0]
    softplus = jnp.maximum(z, 0.0) + jnp.log1p(jnp.exp(-jnp.abs(z)))
    rate = LRU_C * softplus
    t = r * rate
    a = jnp.exp2(r * (rate * (-LOG2_E)))
    u = jnp.sqrt(jnp.tanh(t) * (a * a + 1.0)) * ig * xc
    return a, u


def _scan_tasks(a, u, carry, abuf, ubuf, hbuf, *, T, reverse):
    L = _seg_len(T)
    rows = V7X_SUBLANES * L
    for c, (a_c, u_c) in enumerate(zip(_lane_slabs(a), _lane_slabs(u))):
        abuf[c, 0:T, :] = a_c
        ubuf[c, 0:T, :] = u_c
        abuf[c, T:rows, :] = jnp.ones((rows - T, V7X_LANES), F32)
        ubuf[c, T:rows, :] = jnp.zeros((rows - T, V7X_LANES), F32)
    order = range(L - 1, -1, -1) if reverse else range(L)
    seg = lambda buf, c, j: buf[c, pl.ds(j, V7X_SUBLANES, stride=L), :]
    finals = [None] * N_SLAB

    def slab_task(c, cin):
        sub = lax.broadcasted_iota(jnp.int32, (V7X_SUBLANES, V7X_LANES), 0)
        h = jnp.zeros((V7X_SUBLANES, V7X_LANES), F32)
        p = jnp.ones((V7X_SUBLANES, V7X_LANES), F32)
        for j in order:
            a_j = seg(abuf, c, j)
            h = a_j * h + seg(ubuf, c, j)
            p = a_j * p
        for d in (1, 2, 4):
            sh = V7X_SUBLANES - d if reverse else d
            m = (sub < V7X_SUBLANES - d) if reverse else (sub >= d)
            h = jnp.where(m, p * pltpu.roll(h, sh, axis=0) + h, h)
            p = jnp.where(m, p * pltpu.roll(p, sh, axis=0), p)
        end = h + p * cin
        if reverse:
            h = jnp.where(sub == V7X_SUBLANES - 1, cin, pltpu.roll(end, V7X_SUBLANES - 1, axis=0))
            finals[c] = end[0:1, :]
        else:
            h = jnp.where(sub == 0, cin, pltpu.roll(end, 1, axis=0))
            finals[c] = end[V7X_SUBLANES - 1:V7X_SUBLANES, :]
        for j in order:
            h = seg(abuf, c, j) * h + seg(ubuf, c, j)
            hbuf[c, pl.ds(j, V7X_SUBLANES, stride=L), :] = h

    tasks = [functools.partial(slab_task, c, cin) for c, cin in enumerate(_lane_slabs(carry))]

    def finish():
        h_all = jnp.concatenate([hbuf[c, 0:T, :] for c in range(N_SLAB)], axis=1)
        return h_all, jnp.concatenate(finals, axis=1)

    return tasks, finish


def _scan_segments(a, u, carry, abuf, ubuf, hbuf, *, T, reverse):
    tasks, finish = _scan_tasks(a, u, carry, abuf, ubuf, hbuf, T=T, reverse=reverse)
    for task in tasks:
        task()
    return finish()


CONV_SEGMENTS = 2 * V7X_SUBLANES


def _conv_seg_len(T):
    L = (T + 2 * CONV_HALO) // CONV_SEGMENTS
    assert CONV_SEGMENTS * L == T + 2 * CONV_HALO and L % 8 != 0 and L >= CONV_K // 2, (T, L)
    return L


def _dwconv_tasks(ybuf, yext, wext, obuf, w_ref, b_ref, *, T, K, chunks):
    L = _conv_seg_len(T)
    half = K // 2
    hi = V7X_SUBLANES * L

    def stage(c):
        lanes = slice(c * V7X_LANES, (c + 1) * V7X_LANES)
        for k in range(K):
            wext[c, k] = jnp.broadcast_to(w_ref[0, k:k + 1, lanes], (CONV_SEGMENTS, V7X_LANES)).astype(BF16)
        for j in range(L):
            v = jnp.concatenate([ybuf[c, pl.ds(j, V7X_SUBLANES, stride=L), :],
                                 ybuf[c, pl.ds(hi + j, V7X_SUBLANES, stride=L), :]], axis=0)
            yext[c, half + j] = v.astype(BF16)
            if j >= L - half:
                yext[c, half + j - L] = pltpu.roll(v, 1, axis=0).astype(BF16)
            if j < half:
                yext[c, half + j + L] = pltpu.roll(v, CONV_SEGMENTS - 1, axis=0).astype(BF16)

    def taps(c, j0, j1):
        lanes = slice(c * V7X_LANES, (c + 1) * V7X_LANES)
        bc = jnp.broadcast_to(b_ref[0, :, lanes], (CONV_SEGMENTS, V7X_LANES))
        for j in range(j0, j1):
            acc = jnp.zeros((CONV_SEGMENTS, V7X_LANES), F32)
            for k in range(K):
                acc = acc + yext[c, j + k].astype(F32) * wext[c, k].astype(F32)
            acc = acc + bc
            obuf[c, pl.ds(j, V7X_SUBLANES, stride=L), :] = acc[0:V7X_SUBLANES]
            obuf[c, pl.ds(hi + j, V7X_SUBLANES, stride=L), :] = acc[V7X_SUBLANES:]

    step = -(-L // chunks)
    tasks = [functools.partial(stage, c) for c in range(N_SLAB)]
    for c in range(N_SLAB):
        tasks += [functools.partial(taps, c, j0, min(j0 + step, L)) for j0 in range(0, L, step)]

    def finish():
        return jnp.concatenate([obuf[c, 0:T, :] for c in range(N_SLAB)], axis=1)

    return tasks, finish


def _lru_scratch(T):
    rows = V7X_SUBLANES * _seg_len(T)
    return [pltpu.VMEM((N_SLAB, rows, V7X_LANES), F32) for _ in range(3)]


def _lru_bwd_kernel(*refs, T, halo, mod_row):
    if halo:
        (x_ref, xh_ref, mods_ref, wlo_ref, whi_ref, cw_ref, cb_ref, wax_ref, bax_ref, lam_ref, seed_ref,
         hb_ref, st_ref, buf_ref, carry_ref, abuf, ubuf, hbuf) = refs
    else:
        (x_ref, mods_ref, wlo_ref, whi_ref, cw_ref, cb_ref, wax_ref, bax_ref, lam_ref, seed_ref,
         hb_ref, st_ref, buf_ref, carry_ref, abuf, ubuf, hbuf) = refs
    i = pl.program_id(1)
    mod = _mod_row(mods_ref, pl.program_id(0) if mod_row is None else mod_row)

    @pl.when(i == 0)
    def _():
        carry_ref[...] = seed_ref[0]

    def x_lru(xv):
        h = _ln_mod(xv, mod)
        return jnp.concatenate([_dot(h, wlo_ref[0]), _dot(h, whi_ref[0])], axis=1)

    xl = x_lru(x_ref[0])
    buf_ref[0:T, :] = xl
    if halo:
        buf_ref[T:T + V7X_SUBLANES, :] = jnp.where(i == 0, 0.0, x_lru(xh_ref[0]))
    else:
        buf_ref[T:T + V7X_SUBLANES, :] = jnp.zeros((V7X_SUBLANES, BRANCH_W), F32)
    xc = cb_ref[0] + cw_ref[0, 0:1, :] * xl
    for k in range(1, LRU_CONV_K):
        xc = xc + cw_ref[0, k:k + 1, :] * buf_ref[k:k + T, :]
    a, u = _rglru_gates(xc, wax_ref, bax_ref, lam_ref)
    h, final = _scan_segments(a, u, carry_ref[...], abuf, ubuf, hbuf, T=T, reverse=True)
    hb_ref[0] = h
    carry_ref[...] = final
    st_ref[0] = final


def _lru_bwd_call(x, mods, P, seed, *, l, T, mod_row):
    B, S, _ = x.shape
    nT = S // T
    halo = nT > 1
    W = BRANCH_W
    d = 2 * l + 1
    rev = lambda b, i: (b, nT - 1 - i, 0)
    lay = lambda b, i: (l, 0, 0)
    dirn = lambda b, i: (d, 0, 0)
    in_specs = [pl.BlockSpec((1, T, D_MODEL), rev)]
    args = [x]
    if halo:
        hb = T // V7X_SUBLANES
        last = S // V7X_SUBLANES - 1
        in_specs.append(pl.BlockSpec((1, V7X_SUBLANES, D_MODEL),
                                     lambda b, i: (b, jnp.minimum((nT - i) * hb, last), 0)))
        args.append(x)
    half = W // 2
    in_specs += [
        pl.BlockSpec((1, V7X_SUBLANES, 3 * D_MODEL), lay),
        pl.BlockSpec((1, D_MODEL, half), lambda b, i: (l, 0, COL_XLRU // half)),
        pl.BlockSpec((1, D_MODEL, half), lambda b, i: (l, 0, COL_XLRU // half + 1)),
        pl.BlockSpec((1, LRU_CONV_K, W), dirn),
        pl.BlockSpec((1, 1, W), dirn),
        pl.BlockSpec((1, 2 * LRU_TILES, MXU_TILE, MXU_TILE), lambda b, i, d=dirn: d(b, i) + (0,)),
        pl.BlockSpec((1, 1, 2 * W), dirn),
        pl.BlockSpec((1, 1, W), dirn),
        pl.BlockSpec((1, 1, W), lambda b, i: (b, 0, 0)),
    ]
    args += [mods, P["w_in"], P["w_in"], P["lru_cw"], P["lru_cb"], P["lru_wax"], P["lru_bax"], P["lru_lam"], seed]
    return pl.pallas_call(
        functools.partial(_lru_bwd_kernel, T=T, halo=halo, mod_row=mod_row),
        grid=(B, nT),
        in_specs=in_specs,
        out_specs=[pl.BlockSpec((1, T, W), rev), pl.BlockSpec((1, 1, W), lambda b, i: (b, 0, 0))],
        out_shape=[jax.ShapeDtypeStruct((B, S, W), F32), jax.ShapeDtypeStruct((B, 1, W), F32)],
        scratch_shapes=[pltpu.VMEM((T + V7X_SUBLANES, W), F32), pltpu.VMEM((1, W), F32)] + _lru_scratch(T),
        compiler_params=pltpu.CompilerParams(
            dimension_semantics=("arbitrary", "arbitrary"), vmem_limit_bytes=VMEM_LIMIT_BYTES),
        name="lru_bwd",
    )(*args)


def _rope(x, cos, sin_signed, even):
    outs = []
    for xs in _lane_slabs(x):
        up = pltpu.roll(xs, V7X_LANES - HEAD_DIM // 4, axis=1)
        dn = pltpu.roll(xs, HEAD_DIM // 4, axis=1)
        outs.append(xs * cos + jnp.where(even, up, dn) * sin_signed)
    return outs[0] if len(outs) == 1 else jnp.concatenate(outs, axis=1)


def _mix_kernel(*refs, T, S, local, mod_row):
    W = BRANCH_W
    if local:
        (x_ref, xl_ref, xr_ref, cos_ref, cosl_ref, cosr_ref, sin_ref, sinl_ref, sinr_ref, bias_ref,
         kc_ref, vc_ref, hb_ref, mods_ref, win_ref, sink_ref,
         dww_ref, dwb_ref, clg_ref, clb_ref,
         lcw_ref, lcb_ref, wax_ref, bax_ref, lam_ref, seed_ref,
         y_ref, st_ref,
         hext_ref, kext_ref, vext_ref, yatt_ref, cbuf, yext, wext, obuf, lbuf_ref, carry_ref,
         abuf, ubuf, hbuf) = refs
        halo = BLOCK
    else:
        (x_ref, hb_ref, mods_ref, win_ref, sink_ref,
         dww_ref, dwb_ref, clg_ref, clb_ref,
         lcw_ref, lcb_ref, wax_ref, bax_ref, lam_ref, seed_ref,
         y_ref, st_ref, ko_ref, vo_ref,
         hext_ref, kext_ref, vext_ref, yatt_ref, cbuf, yext, wext, obuf, lbuf_ref, carry_ref,
         abuf, ubuf, hbuf) = refs
        halo = 0
    i = pl.program_id(1)
    mod = _mod_row(mods_ref, pl.program_id(0) if mod_row is None else mod_row)
    w_in = lambda c0, c1: win_ref[0, :, c0:c1]

    @pl.when(i == 0)
    def _():
        carry_ref[...] = seed_ref[0]

    hext_ref[halo:halo + T, :] = _ln_mod(x_ref[0], mod)
    if local:
        hext_ref[0:halo, :] = _ln_mod(xl_ref[0], mod)
        hext_ref[halo + T:halo + T + halo, :] = _ln_mod(xr_ref[0], mod)
    hc = hext_ref[halo:halo + T, :]

    state = {}
    if local:
        lane = lax.broadcasted_iota(jnp.int32, (1, V7X_LANES), 1)
        even = (lane & (HEAD_DIM // 4)) == 0

    def with_ones(v):
        ones = jnp.ones((v.shape[0], HEAD_DIM), F32)
        return jnp.concatenate([v[:, :HEAD_DIM], ones, v[:, HEAD_DIM:], ones], axis=1).astype(BF16)

    def proj_kv():
        w_kv = w_in(COL_K, COL_GATT)
        if local:
            parts = ((0, halo, cosl_ref, sinl_ref), (halo, T, cos_ref, sin_ref),
                     (halo + T, halo, cosr_ref, sinr_ref))
            for r0, n, c_ref, s_ref in parts:
                kv = _dot(hext_ref[r0:r0 + n, :], w_kv)
                kext_ref[r0:r0 + n, :] = _rope(kv[:, :KV_W], c_ref[...], s_ref[...], even).astype(BF16)
                vext_ref[r0:r0 + n, :] = with_ones(kv[:, KV_W:])
        else:
            kv = _dot(hc, w_kv)
            kext_ref[...] = kv[:, :KV_W].astype(BF16)
            vext_ref[...] = with_ones(kv[:, KV_W:])
            ko_ref[0] = kext_ref[...]
            vo_ref[0] = vext_ref[...]

    def proj_q():
        q = _dot(hc, w_in(COL_Q, COL_K))
        if local:
            q = _rope(q, cos_ref[...], sin_ref[...], even)
        state["q"] = q * (ATTN_SCALE * LOG2_E)

    def attention(jb, hk):
        q = state["q"]
        r0 = jb * BLOCK
        heads = [GROUP * hk + g for g in range(GROUP)]
        q4 = jnp.concatenate(
            [q[r0:r0 + BLOCK, h * HEAD_DIM:(h + 1) * HEAD_DIM] for h in heads], axis=0).astype(BF16)
        sink = LOG2_E * jnp.concatenate(
            [jnp.broadcast_to(sink_ref[0, h:h + 1, 0:1], (BLOCK, 1)) for h in heads], axis=0)
        ks = slice(hk * HEAD_DIM, (hk + 1) * HEAD_DIM)
        vs = slice(hk * V7X_LANES, (hk + 1) * V7X_LANES)
        if local:
            k_c, v_c = kc_ref[0, :, ks], vc_ref[0, :, vs]
        else:
            k_c, v_c = kext_ref[:, ks], vext_ref[:, vs]
        s_ctx = _dot_nt(q4, k_c)
        m = jnp.maximum(jnp.max(s_ctx, axis=-1, keepdims=True), sink)
        if local:
            pos0 = i * T + r0
            bidx = jnp.where(pos0 == 0, 1, jnp.where(pos0 == S - BLOCK, 2, 0))
            bias = bias_ref[bidx]
            k_w = kext_ref[r0:r0 + 3 * BLOCK, ks]
            v_w = vext_ref[r0:r0 + 3 * BLOCK, vs]
            s_loc = _dot_nt(q4, k_w)
            s_loc = jnp.concatenate(
                [s_loc[g * BLOCK:(g + 1) * BLOCK] + bias for g in range(GROUP)], axis=0)
            m = jnp.maximum(m, jnp.max(s_loc, axis=-1, keepdims=True))
        o = _dot(jnp.exp2(s_ctx - m).astype(BF16), v_c)
        if local:
            o = o + _dot(jnp.exp2(s_loc - m).astype(BF16), v_w)
        den = pltpu.roll(o, HEAD_DIM, axis=1) + jnp.exp2(sink - m)
        o = o / den
        for g, h in enumerate(heads):
            yatt_ref[r0:r0 + BLOCK, h * HEAD_DIM:(h + 1) * HEAD_DIM] = o[g * BLOCK:(g + 1) * BLOCK, :HEAD_DIM]

    n_ext = T + 2 * CONV_HALO
    zeros = jnp.zeros((CONV_HALO, W), F32)
    h_wide = hext_ref[halo - CONV_HALO:halo + T + CONV_HALO, :] if local else hc
    if local:
        pos = i * T - CONV_HALO + lax.broadcasted_iota(jnp.int32, (n_ext, 1), 0)
        inside = (pos >= 0) & (pos < S)
    conv_tasks, conv_finish = _dwconv_tasks(cbuf, yext, wext, obuf, dww_ref, dwb_ref, T=T, K=CONV_K, chunks=2)
    lru_tasks, lru_finish = [], []

    def proj_glu():
        ag = _dot(h_wide, w_in(COL_UCONV, COL_GCONV))
        glu = ag[:, 0:W] * _sigmoid(ag[:, W:2 * W])
        if local:
            glu = jnp.where(inside, glu, 0.0)
            parts = (glu[CONV_HALO:CONV_HALO + T], glu[CONV_HALO + T:n_ext], glu[0:CONV_HALO])
        else:
            parts = (glu, zeros, zeros)
        for r0, part in zip((0, T, T + CONV_HALO), parts):
            for c, slab in enumerate(_lane_slabs(part)):
                cbuf[c, r0:r0 + part.shape[0], :] = slab
        side.append(conv_tasks)

    def lru_front():
        xc = jnp.broadcast_to(lcb_ref[0], (T, W))
        off = CONV_HALO - (LRU_CONV_K - 1)
        for k in range(LRU_CONV_K):
            xc = xc + lcw_ref[0, k:k + 1, :] * lbuf_ref[off + k:off + k + T, :]
        a, u = _rglru_gates(xc, wax_ref, bax_ref, lam_ref)
        tasks, finish = _scan_tasks(a, u, carry_ref[...], abuf, ubuf, hbuf, T=T, reverse=False)
        lru_tasks.extend(tasks)
        lru_finish.append(finish)

    def proj_gate_lru():
        gx = _dot(h_wide, w_in(COL_GCONV, COL_GLRU))
        if local:
            lbuf_ref[...] = jnp.where(inside, gx[:, W:2 * W], 0.0)
            state["g_conv"] = gx[CONV_HALO:CONV_HALO + T, 0:W]
        else:
            lbuf_ref[0:CONV_HALO, :] = zeros
            lbuf_ref[CONV_HALO + T:n_ext, :] = zeros
            lbuf_ref[CONV_HALO:CONV_HALO + T, :] = gx[:, W:2 * W]
            state["g_conv"] = gx[:, 0:W]
        lru_tasks.append(lru_front)
        side.append(lru_tasks)

    side = []
    for proj in (proj_kv, proj_q, proj_glu, proj_gate_lru):
        proj()
    blocks = [(jb, hk) for jb in range(T // BLOCK) for hk in range(N_KV_HEADS)]
    for n, (jb, hk) in enumerate(blocks):
        attention(jb, hk)
        left = len(blocks) - n
        for queue in side:
            for _ in range(-(-len(queue) // left)):
                queue.pop(0)()
    assert not conv_tasks and not lru_tasks
    g_conv = state["g_conv"]

    g_att = _dot(hc, w_in(COL_GATT, COL_UCONV))
    y_ref[0, :, 0:W] = (yatt_ref[...] * _silu(g_att)).astype(BF16)

    y_conv = _silu(_layer_norm(conv_finish()) * clg_ref[0] + clb_ref[0])
    y_ref[0, :, W:2 * W] = (y_conv * _silu(g_conv)).astype(BF16)

    hf, final = lru_finish[0]()
    carry_ref[...] = final
    st_ref[0] = final
    g_lru = _dot(hc, w_in(COL_GLRU, IN_COLS))
    y_ref[0, :, 2 * W:3 * W] = ((hf + hb_ref[0]) * _silu(g_lru)).astype(BF16)


def _mix_call(x, hb, mods, P, seed, *, l, T, local, mod_row, rope=None, bias=None, kc=None, vc=None):
    B, S, _ = x.shape
    nT = S // T
    W = BRANCH_W
    tile = lambda b, i: (b, i, 0)
    batch = lambda b, i: (b, 0, 0)
    lay = lambda b, i: (l, 0, 0)
    dirn = lambda b, i: (2 * l, 0, 0)
    in_specs, args = [pl.BlockSpec((1, T, D_MODEL), tile)], [x]
    if local:
        nb = T // BLOCK
        last = S // BLOCK - 1
        left = lambda b, i: (b, jnp.maximum(i * nb - 1, 0), 0)
        right = lambda b, i: (b, jnp.minimum((i + 1) * nb, last), 0)
        in_specs += [pl.BlockSpec((1, BLOCK, D_MODEL), left), pl.BlockSpec((1, BLOCK, D_MODEL), right)]
        args += [x, x]
        cos, sin = rope
        for tab in (cos, sin):
            in_specs += [
                pl.BlockSpec((T, V7X_LANES), lambda b, i: (i, 0)),
                pl.BlockSpec((BLOCK, V7X_LANES), lambda b, i: (jnp.maximum(i * nb - 1, 0), 0)),
                pl.BlockSpec((BLOCK, V7X_LANES), lambda b, i: (jnp.minimum((i + 1) * nb, last), 0)),
            ]
            args += [tab, tab, tab]
        C = kc.shape[1]
        in_specs += [
            pl.BlockSpec((3, BLOCK, 3 * BLOCK), lambda b, i: (0, 0, 0)),
            pl.BlockSpec((1, C, KV_W), batch),
            pl.BlockSpec((1, C, V_EXT_W), batch),
        ]
        args += [bias, kc, vc]
    in_specs += [
        pl.BlockSpec((1, T, W), tile),
        pl.BlockSpec((1, V7X_SUBLANES, 3 * D_MODEL), lay),
        pl.BlockSpec((1, D_MODEL, IN_COLS), lay),
        pl.BlockSpec((1, N_HEADS, V7X_LANES), lay),
        pl.BlockSpec((1, CONV_K, W), lay),
        pl.BlockSpec((1, 1, W), lay),
        pl.BlockSpec((1, 1, W), lay),
        pl.BlockSpec((1, 1, W), lay),
        pl.BlockSpec((1, LRU_CONV_K, W), dirn),
        pl.BlockSpec((1, 1, W), dirn),
        pl.BlockSpec((1, 2 * LRU_TILES, MXU_TILE, MXU_TILE), lambda b, i, d=dirn: d(b, i) + (0,)),
        pl.BlockSpec((1, 1, 2 * W), dirn),
        pl.BlockSpec((1, 1, W), dirn),
        pl.BlockSpec((1, 1, W), batch),
    ]
    args += [hb, mods, P["w_in"], P["sink"], P["dw_w"], P["dw_b"], P["cl_g"], P["cl_b"],
             P["lru_cw"], P["lru_cb"], P["lru_wax"], P["lru_bax"], P["lru_lam"], seed]
    out_specs = [pl.BlockSpec((1, T, 3 * W), tile), pl.BlockSpec((1, 1, W), batch)]
    out_shape = [jax.ShapeDtypeStruct((B, S, 3 * W), BF16), jax.ShapeDtypeStruct((B, 1, W), F32)]
    if not local:
        out_specs += [pl.BlockSpec((1, T, KV_W), tile), pl.BlockSpec((1, T, V_EXT_W), tile)]
        out_shape += [jax.ShapeDtypeStruct((B, S, KV_W), BF16), jax.ShapeDtypeStruct((B, S, V_EXT_W), BF16)]
    halo = BLOCK if local else 0
    n_ext = T + 2 * CONV_HALO
    L = _conv_seg_len(T)
    scratch = [
        pltpu.VMEM((T + 2 * halo, D_MODEL), BF16),
        pltpu.VMEM((T + 2 * halo, KV_W), BF16),
        pltpu.VMEM((T + 2 * halo, V_EXT_W), BF16),
        pltpu.VMEM((T, W), F32),
        pltpu.VMEM((N_SLAB, n_ext, V7X_LANES), F32),
        pltpu.VMEM((N_SLAB, L + 2 * (CONV_K // 2), CONV_SEGMENTS, V7X_LANES), BF16),
        pltpu.VMEM((N_SLAB, CONV_K, CONV_SEGMENTS, V7X_LANES), BF16),
        pltpu.VMEM((N_SLAB, n_ext, V7X_LANES), F32),
        pltpu.VMEM((n_ext, W), F32),
        pltpu.VMEM((1, W), F32),
    ] + _lru_scratch(T)
    return pl.pallas_call(
        functools.partial(_mix_kernel, T=T, S=S, local=local, mod_row=mod_row),
        grid=(B, nT),
        in_specs=in_specs,
        out_specs=out_specs,
        out_shape=out_shape,
        scratch_shapes=scratch,
        compiler_params=pltpu.CompilerParams(
            dimension_semantics=("arbitrary", "arbitrary"), vmem_limit_bytes=VMEM_LIMIT_BYTES),
        name="mix_local" if local else "mix_ctx",
    )(*args)


def _merge_kernel(x_ref, y_ref, mods_ref, wgate_ref, bgate_ref, wbr_ref, wout_ref, lng_ref, lnb_ref, o_ref, *,
                  mod_row):
    x = x_ref[0]
    mod = _mod_row(mods_ref, pl.program_id(0) if mod_row is None else mod_row)
    h = _ln_mod(x, mod)
    m = None
    for n in range(N_BRANCH):
        cols = slice(n * D_MODEL, (n + 1) * D_MODEL)
        g = _sigmoid(_dot(h, wgate_ref[0, :, cols]) + bgate_ref[0, :, cols])
        p = _dot(y_ref[0, :, n * BRANCH_W:(n + 1) * BRANCH_W], wbr_ref[0, n])
        m = p * g if m is None else m + p * g
    out = _dot(m.astype(BF16), wout_ref[0])
    gate = mod[:, 2 * D_MODEL:3 * D_MODEL]
    z = DEEPNORM_ALPHA * x + gate * out
    o_ref[0] = _layer_norm(z) * lng_ref[0] + lnb_ref[0]


def _merge_call(x, y, mods, P, *, l, T, mod_row):
    B, S, _ = x.shape
    nT = S // T
    tile = lambda b, i: (b, i, 0)
    lay = lambda b, i: (l, 0, 0)
    return pl.pallas_call(
        functools.partial(_merge_kernel, mod_row=mod_row),
        grid=(B, nT),
        in_specs=[
            pl.BlockSpec((1, T, D_MODEL), tile),
            pl.BlockSpec((1, T, 3 * BRANCH_W), tile),
            pl.BlockSpec((1, V7X_SUBLANES, 3 * D_MODEL), lay),
            pl.BlockSpec((1, D_MODEL, 3 * D_MODEL), lay),
            pl.BlockSpec((1, 1, 3 * D_MODEL), lay),
            pl.BlockSpec((1, N_BRANCH, BRANCH_W, D_MODEL), lambda b, i: (l, 0, 0, 0)),
            pl.BlockSpec((1, D_MODEL, D_MODEL), lay),
            pl.BlockSpec((1, 1, D_MODEL), lay),
            pl.BlockSpec((1, 1, D_MODEL), lay),
        ],
        out_specs=pl.BlockSpec((1, T, D_MODEL), tile),
        out_shape=jax.ShapeDtypeStruct((B, S, D_MODEL), F32),
        compiler_params=pltpu.CompilerParams(
            dimension_semantics=("arbitrary", "arbitrary"), vmem_limit_bytes=VMEM_LIMIT_BYTES),
        name="merge",
    )(x, y, mods, P["w_gate"], P["b_gate"], P["w_branch"], P["w_out"], P["ln_g"], P["ln_b"])


def _rope_tables(S):
    quarter = HEAD_DIM // 4
    rows = S // GRID_W
    inv = ROPE_BASE ** (-jnp.arange(quarter, dtype=F32) / quarter)
    ang_r = jnp.arange(rows, dtype=F32)[:, None] * inv
    ang_c = jnp.arange(GRID_W, dtype=F32)[:, None] * inv

    def table(fn, signs):
        r = jnp.broadcast_to(fn(ang_r)[:, None, :], (rows, GRID_W, quarter))
        c = jnp.broadcast_to(fn(ang_c)[None, :, :], (rows, GRID_W, quarter))
        head = [signs[0] * r, signs[1] * r, signs[0] * c, signs[1] * c]
        return jnp.concatenate(head * (V7X_LANES // HEAD_DIM), axis=-1).reshape(S, V7X_LANES)

    return table(jnp.cos, (1.0, 1.0)), table(jnp.sin, (-1.0, 1.0))


def _window_bias():
    r = np.arange(BLOCK)[:, None]
    j = np.arange(3 * BLOCK)[None, :]
    band = np.abs(j - BLOCK - r) <= WINDOW
    first = band & (j >= BLOCK)
    last = band & (j < 2 * BLOCK)
    return jnp.asarray(np.where(np.stack([band, first, last]), 0.0, -np.inf), dtype=F32)


def _gate_tiles(w):
    blk = w.shape[-1]
    per = MXU_TILE // blk
    wt = w.astype(BF16).reshape(w.shape[:2] + (LRU_TILES, per, blk, blk))
    lead = ((0, 0),) * 3
    return sum(jnp.pad(wt[:, :, :, n], lead + ((n * blk, MXU_TILE - (n + 1) * blk),) * 2) for n in range(per))


def _pack_params(w_in, attn_sink, conv_dw_w, conv_dw_b, conv_ln_g, conv_ln_b, lru_conv_w, lru_conv_b,
                 lru_w_a, lru_b_a, lru_w_x, lru_b_x, lru_lambda, w_branch, w_gate, b_gate, w_out, ln_g, ln_b):
    depth = w_in.shape[0]
    W = BRANCH_W
    rows = lambda v: v.reshape(-1, 1, v.shape[-1])
    return {
        "w_in": w_in.astype(BF16),
        "sink": jnp.broadcast_to(attn_sink[:, :, None], (depth, N_HEADS, V7X_LANES)),
        "dw_w": conv_dw_w, "dw_b": rows(conv_dw_b), "cl_g": rows(conv_ln_g), "cl_b": rows(conv_ln_b),
        "lru_cw": lru_conv_w.reshape(2 * depth, LRU_CONV_K, W),
        "lru_cb": rows(lru_conv_b),
        "lru_wax": jnp.concatenate([_gate_tiles(lru_w_a), _gate_tiles(lru_w_x)], axis=2)
                      .reshape(2 * depth, 2 * LRU_TILES, MXU_TILE, MXU_TILE),
        "lru_bax": rows(jnp.concatenate([lru_b_a, lru_b_x], axis=-1)),
        "lru_lam": rows(lru_lambda),
        "w_gate": w_gate.astype(BF16), "b_gate": rows(b_gate),
        "w_branch": w_branch.astype(BF16), "w_out": w_out.astype(BF16),
        "ln_g": rows(ln_g), "ln_b": rows(ln_b),
    }


TILE_T = 512
LRU_TILE_T = 1024


def kernel(x, c, ctx, c_ctx, w_ada, b_ada, w_in, attn_sink, conv_dw_w, conv_dw_b, conv_ln_g, conv_ln_b,
           lru_conv_w, lru_conv_b, lru_w_a, lru_b_a, lru_w_x, lru_b_x, lru_lambda,
           w_branch, w_gate, b_gate, w_out, ln_g, ln_b):
    B, S, D = x.shape
    C = ctx.shape[1]
    depth = w_ada.shape[0]
    assert D == D_MODEL and S % TILE_T == 0 and S % LRU_TILE_T == 0 and S >= 2 * BLOCK and B + 1 <= V7X_SUBLANES

    c_rows = jnp.concatenate([c, c_ctx[None, :], jnp.zeros((V7X_SUBLANES - B - 1, D), F32)], axis=0)
    mods = _mods_call(c_rows, w_ada, b_ada)
    P = _pack_params(w_in, attn_sink, conv_dw_w, conv_dw_b, conv_ln_g, conv_ln_b, lru_conv_w, lru_conv_b,
                     lru_w_a, lru_b_a, lru_w_x, lru_b_x, lru_lambda, w_branch, w_gate, b_gate, w_out, ln_g, ln_b)
    rope = _rope_tables(S)
    bias = _window_bias()
    zero_seed = jnp.zeros((B, 1, BRANCH_W), F32)

    for l in range(depth):
        hb_c, seed_b = _lru_bwd_call(ctx, mods, P, zero_seed, l=l, T=C, mod_row=B)
        y_c, seed_f, kc, vc = _mix_call(ctx, hb_c, mods, P, zero_seed, l=l, T=C, local=False, mod_row=B)

        hb = _lru_bwd_call(x, mods, P, seed_b, l=l, T=LRU_TILE_T, mod_row=None)[0]
        y = _mix_call(x, hb, mods, P, seed_f, l=l, T=TILE_T, local=True, mod_row=None,
                      rope=rope, bias=bias, kc=kc, vc=vc)[0]
        x = _merge_call(x, y, mods, P, l=l, T=TILE_T, mod_row=None)
        if l != depth - 1:
            ctx = _merge_call(ctx, y_c, mods, P, l=l, T=C, mod_row=B)
    return x
```

```python
import functools

import jax
import jax.numpy as jnp
import numpy as np
from jax import lax
from jax.experimental import pallas as pl
from jax.experimental.pallas import tpu as pltpu

F32 = jnp.float32
BF16 = jnp.bfloat16

D_MODEL = 1024
DEPTH = 4
GRID_W = 64
BRANCH_W = D_MODEL // 2
N_BRANCH = 3
HEAD_DIM = 64
N_HEADS = BRANCH_W // HEAD_DIM
N_KV_HEADS = N_HEADS // 4
GROUP = N_HEADS // N_KV_HEADS
KV_W = N_KV_HEADS * HEAD_DIM
V_EXT_W = N_KV_HEADS * 128
WINDOW = 128
BLOCK = 128
ROPE_BASE = 10000.0
CONV_K = 31
CONV_HALO = 16
LRU_BLOCKS = 8
LRU_CONV_K = 4
LRU_C = 8.0
DEEPNORM_ALPHA = (2 * DEPTH) ** 0.25
LN_EPS = 1e-6
ATTN_SCALE = HEAD_DIM ** -0.5

_IN_SIZES = (N_HEADS * HEAD_DIM, KV_W, KV_W, BRANCH_W, 2 * BRANCH_W, BRANCH_W, BRANCH_W, BRANCH_W)
_IN_OFF = np.cumsum((0,) + _IN_SIZES).tolist()
IN_COLS = _IN_OFF[-1]
COL_Q, COL_K, COL_V, COL_GATT, COL_UCONV, COL_GCONV, COL_XLRU, COL_GLRU = _IN_OFF[:-1]

V7X_LANES = 128
V7X_SUBLANES = 8
VMEM_LIMIT_BYTES = 56 * 1024 * 1024
N_SLAB = BRANCH_W // V7X_LANES
MXU_TILE = 256
LRU_TILES = BRANCH_W // MXU_TILE
LOG2_E = 1.4426950408889634


def _seg_len(T):
    L = (T + 2 * CONV_HALO) // V7X_SUBLANES
    assert V7X_SUBLANES * L == T + 2 * CONV_HALO and L % 8 == 4, (T, L)
    return L


def _layer_norm(x):
    mu = jnp.mean(x, axis=-1, keepdims=True)
    xc = x - mu
    var = jnp.mean(xc * xc, axis=-1, keepdims=True)
    return xc * lax.rsqrt(var + LN_EPS)


def _sigmoid(x):
    return jax.nn.sigmoid(x)


def _silu(x):
    return x * jax.nn.sigmoid(x)


def _dot(a, b):
    return jnp.dot(a, b, preferred_element_type=F32)


def _dot_nt(a, b):
    return lax.dot_general(a, b, (((1,), (1,)), ((), ())), preferred_element_type=F32)


def _ln_mod(x, mod):
    shift = mod[:, :D_MODEL]
    scale = mod[:, D_MODEL:2 * D_MODEL]
    return (_layer_norm(x) * (1.0 + scale) + shift).astype(BF16)


def _mod_row(mods_ref, row):
    if isinstance(row, int):
        return mods_ref[0, row:row + 1, :]
    return mods_ref[0, pl.ds(row, 1), :]


def _lane_slabs(x):
    return [x[:, c * V7X_LANES:(c + 1) * V7X_LANES] for c in range(x.shape[1] // V7X_LANES)]


def _mods_kernel(c_ref, w_ref, b_ref, o_ref):
    s = _silu(c_ref[...])
    o_ref[0] = _dot(s.astype(BF16), w_ref[0].astype(BF16)) + b_ref[0]


def _mods_call(c_rows, w_ada, b_ada):
    depth = w_ada.shape[0]
    ncol = 3
    return pl.pallas_call(
        _mods_kernel,
        grid=(depth, ncol),
        in_specs=[
            pl.BlockSpec((V7X_SUBLANES, D_MODEL), lambda l, j: (0, 0)),
            pl.BlockSpec((1, D_MODEL, D_MODEL), lambda l, j: (l, 0, j)),
            pl.BlockSpec((1, 1, D_MODEL), lambda l, j: (l, 0, j)),
        ],
        out_specs=pl.BlockSpec((1, V7X_SUBLANES, D_MODEL), lambda l, j: (l, 0, j)),
        out_shape=jax.ShapeDtypeStruct((depth, V7X_SUBLANES, 3 * D_MODEL), F32),
        compiler_params=pltpu.CompilerParams(
            dimension_semantics=("arbitrary", "arbitrary"), vmem_limit_bytes=VMEM_LIMIT_BYTES),
        name="adaln_mods",
    )(c_rows, w_ada, b_ada.reshape(depth, 1, 3 * D_MODEL))


def _rglru_gates(xc, wax_ref, bax_ref, lam_ref):
    xb = xc.astype(BF16)
    cols = [xb[:, t * MXU_TILE:(t + 1) * MXU_TILE] for t in range(LRU_TILES)]
    gate = lambda g: jnp.concatenate(
        [_dot(cols[t], wax_ref[0, g * LRU_TILES + t]) for t in range(LRU_TILES)], axis=1)
    r = _sigmoid(gate(0) + bax_ref[0, :, :BRANCH_W])
    ig = _sigmoid(gate(1) + bax_ref[0, :, BRANCH_W:])
    z = -lam_ref[0]
    softplus = jnp.maximum(z, 0.0) + jnp.log1p(jnp.exp(-jnp.abs(z)))
    rate = LRU_C * softplus
    t = r * rate
    a = jnp.exp2(r * (rate * (-LOG2_E)))
    u = jnp.sqrt(jnp.tanh(t) * (a * a + 1.0)) * ig * xc
    return a, u


def _scan_tasks(a, u, carry, abuf, ubuf, hbuf, *, T, reverse):
    L = _seg_len(T)
    rows = V7X_SUBLANES * L
    for c, (a_c, u_c) in enumerate(zip(_lane_slabs(a), _lane_slabs(u))):
        abuf[c, 0:T, :] = a_c
        ubuf[c, 0:T, :] = u_c
        abuf[c, T:rows, :] = jnp.ones((rows - T, V7X_LANES), F32)
        ubuf[c, T:rows, :] = jnp.zeros((rows - T, V7X_LANES), F32)
    order = range(L - 1, -1, -1) if reverse else range(L)
    seg = lambda buf, c, j: buf[c, pl.ds(j, V7X_SUBLANES, stride=L), :]
    finals = [None] * N_SLAB

    def slab_task(c, cin):
        sub = lax.broadcasted_iota(jnp.int32, (V7X_SUBLANES, V7X_LANES), 0)
        h = jnp.zeros((V7X_SUBLANES, V7X_LANES), F32)
        p = jnp.ones((V7X_SUBLANES, V7X_LANES), F32)
        for j in order:
            a_j = seg(abuf, c, j)
            h = a_j * h + seg(ubuf, c, j)
            p = a_j * p
        for d in (1, 2, 4):
            sh = V7X_SUBLANES - d if reverse else d
            m = (sub < V7X_SUBLANES - d) if reverse else (sub >= d)
            h = jnp.where(m, p * pltpu.roll(h, sh, axis=0) + h, h)
            p = jnp.where(m, p * pltpu.roll(p, sh, axis=0), p)
        end = h + p * cin
        if reverse:
            h = jnp.where(sub == V7X_SUBLANES - 1, cin, pltpu.roll(end, V7X_SUBLANES - 1, axis=0))
            finals[c] = end[0:1, :]
        else:
            h = jnp.where(sub == 0, cin, pltpu.roll(end, 1, axis=0))
            finals[c] = end[V7X_SUBLANES - 1:V7X_SUBLANES, :]
        for j in order:
            h = seg(abuf, c, j) * h + seg(ubuf, c, j)
            hbuf[c, pl.ds(j, V7X_SUBLANES, stride=L), :] = h

    tasks = [functools.partial(slab_task, c, cin) for c, cin in enumerate(_lane_slabs(carry))]

    def finish():
        h_all = jnp.concatenate([hbuf[c, 0:T, :] for c in range(N_SLAB)], axis=1)
        return h_all, jnp.concatenate(finals, axis=1)

    return tasks, finish


def _scan_segments(a, u, carry, abuf, ubuf, hbuf, *, T, reverse):
    tasks, finish = _scan_tasks(a, u, carry, abuf, ubuf, hbuf, T=T, reverse=reverse)
    for task in tasks:
        task()
    return finish()


CONV_SEGMENTS = 2 * V7X_SUBLANES


def _conv_seg_len(T):
    L = (T + 2 * CONV_HALO) // CONV_SEGMENTS
    assert CONV_SEGMENTS * L == T + 2 * CONV_HALO and L % 8 != 0 and L >= CONV_K // 2, (T, L)
    return L


def _dwconv_tasks(ybuf, yext, wext, obuf, w_ref, b_ref, *, T, K, chunks):
    L = _conv_seg_len(T)
    half = K // 2
    hi = V7X_SUBLANES * L

    def stage(c):
        lanes = slice(c * V7X_LANES, (c + 1) * V7X_LANES)
        for k in range(K):
            wext[c, k] = jnp.broadcast_to(w_ref[0, k:k + 1, lanes], (CONV_SEGMENTS, V7X_LANES)).astype(BF16)
        for j in range(L):
            v = jnp.concatenate([ybuf[c, pl.ds(j, V7X_SUBLANES, stride=L), :],
                                 ybuf[c, pl.ds(hi + j, V7X_SUBLANES, stride=L), :]], axis=0)
            yext[c, half + j] = v.astype(BF16)
            if j >= L - half:
                yext[c, half + j - L] = pltpu.roll(v, 1, axis=0).astype(BF16)
            if j < half:
                yext[c, half + j + L] = pltpu.roll(v, CONV_SEGMENTS - 1, axis=0).astype(BF16)

    def taps(c, j0, j1):
        lanes = slice(c * V7X_LANES, (c + 1) * V7X_LANES)
        bc = jnp.broadcast_to(b_ref[0, :, lanes], (CONV_SEGMENTS, V7X_LANES))
        for j in range(j0, j1):
            acc = jnp.zeros((CONV_SEGMENTS, V7X_LANES), F32)
            for k in range(K):
                acc = acc + yext[c, j + k].astype(F32) * wext[c, k].astype(F32)
            acc = acc + bc
            obuf[c, pl.ds(j, V7X_SUBLANES, stride=L), :] = acc[0:V7X_SUBLANES]
            obuf[c, pl.ds(hi + j, V7X_SUBLANES, stride=L), :] = acc[V7X_SUBLANES:]

    step = -(-L // chunks)
    tasks = [functools.partial(stage, c) for c in range(N_SLAB)]
    for c in range(N_SLAB):
        tasks += [functools.partial(taps, c, j0, min(j0 + step, L)) for j0 in range(0, L, step)]

    def finish():
        return jnp.concatenate([obuf[c, 0:T, :] for c in range(N_SLAB)], axis=1)

    return tasks, finish


def _lru_scratch(T):
    rows = V7X_SUBLANES * _seg_len(T)
    return [pltpu.VMEM((N_SLAB, rows, V7X_LANES), F32) for _ in range(3)]


def _lru_bwd_kernel(*refs, T, halo, mod_row):
    if halo:
        (x_ref, xh_ref, mods_ref, wlo_ref, whi_ref, cw_ref, cb_ref, wax_ref, bax_ref, lam_ref, seed_ref,
         hb_ref, st_ref, buf_ref, carry_ref, abuf, ubuf, hbuf) = refs
    else:
        (x_ref, mods_ref, wlo_ref, whi_ref, cw_ref, cb_ref, wax_ref, bax_ref, lam_ref, seed_ref,
         hb_ref, st_ref, buf_ref, carry_ref, abuf, ubuf, hbuf) = refs
    i = pl.program_id(1)
    mod = _mod_row(mods_ref, pl.program_id(0) if mod_row is None else mod_row)

    @pl.when(i == 0)
    def _():
        carry_ref[...] = seed_ref[0]

    def x_lru(xv):
        h = _ln_mod(xv, mod)
        return jnp.concatenate([_dot(h, wlo_ref[0]), _dot(h, whi_ref[0])], axis=1)

    xl = x_lru(x_ref[0])
    buf_ref[0:T, :] = xl
    if halo:
        buf_ref[T:T + V7X_SUBLANES, :] = jnp.where(i == 0, 0.0, x_lru(xh_ref[0]))
    else:
        buf_ref[T:T + V7X_SUBLANES, :] = jnp.zeros((V7X_SUBLANES, BRANCH_W), F32)
    xc = cb_ref[0] + cw_ref[0, 0:1, :] * xl
    for k in range(1, LRU_CONV_K):
        xc = xc + cw_ref[0, k:k + 1, :] * buf_ref[k:k + T, :]
    a, u = _rglru_gates(xc, wax_ref, bax_ref, lam_ref)
    h, final = _scan_segments(a, u, carry_ref[...], abuf, ubuf, hbuf, T=T, reverse=True)
    hb_ref[0] = h
    carry_ref[...] = final
    st_ref[0] = final


def _lru_bwd_call(x, mods, P, seed, *, l, T, mod_row):
    B, S, _ = x.shape
    nT = S // T
    halo = nT > 1
    W = BRANCH_W
    d = 2 * l + 1
    rev = lambda b, i: (b, nT - 1 - i, 0)
    lay = lambda b, i: (l, 0, 0)
    dirn = lambda b, i: (d, 0, 0)
    in_specs = [pl.BlockSpec((1, T, D_MODEL), rev)]
    args = [x]
    if halo:
        hb = T // V7X_SUBLANES
        last = S // V7X_SUBLANES - 1
        in_specs.append(pl.BlockSpec((1, V7X_SUBLANES, D_MODEL),
                                     lambda b, i: (b, jnp.minimum((nT - i) * hb, last), 0)))
        args.append(x)
    half = W // 2
    in_specs += [
        pl.BlockSpec((1, V7X_SUBLANES, 3 * D_MODEL), lay),
        pl.BlockSpec((1, D_MODEL, half), lambda b, i: (l, 0, COL_XLRU // half)),
        pl.BlockSpec((1, D_MODEL, half), lambda b, i: (l, 0, COL_XLRU // half + 1)),
        pl.BlockSpec((1, LRU_CONV_K, W), dirn),
        pl.BlockSpec((1, 1, W), dirn),
        pl.BlockSpec((1, 2 * LRU_TILES, MXU_TILE, MXU_TILE), lambda b, i, d=dirn: d(b, i) + (0,)),
        pl.BlockSpec((1, 1, 2 * W), dirn),
        pl.BlockSpec((1, 1, W), dirn),
        pl.BlockSpec((1, 1, W), lambda b, i: (b, 0, 0)),
    ]
    args += [mods, P["w_in"], P["w_in"], P["lru_cw"], P["lru_cb"], P["lru_wax"], P["lru_bax"], P["lru_lam"], seed]
    return pl.pallas_call(
        functools.partial(_lru_bwd_kernel, T=T, halo=halo, mod_row=mod_row),
        grid=(B, nT),
        in_specs=in_specs,
        out_specs=[pl.BlockSpec((1, T, W), rev), pl.BlockSpec((1, 1, W), lambda b, i: (b, 0, 0))],
        out_shape=[jax.ShapeDtypeStruct((B, S, W), F32), jax.ShapeDtypeStruct((B, 1, W), F32)],
        scratch_shapes=[pltpu.VMEM((T + V7X_SUBLANES, W), F32), pltpu.VMEM((1, W), F32)] + _lru_scratch(T),
        compiler_params=pltpu.CompilerParams(
            dimension_semantics=("arbitrary", "arbitrary"), vmem_limit_bytes=VMEM_LIMIT_BYTES),
        name="lru_bwd",
    )(*args)


def _rope(x, cos, sin_signed, even):
    outs = []
    for xs in _lane_slabs(x):
        up = pltpu.roll(xs, V7X_LANES - HEAD_DIM // 4, axis=1)
        dn = pltpu.roll(xs, HEAD_DIM // 4, axis=1)
        outs.append(xs * cos + jnp.where(even, up, dn) * sin_signed)
    return outs[0] if len(outs) == 1 else jnp.concatenate(outs, axis=1)


def _mix_kernel(*refs, T, S, local, mod_row):
    W = BRANCH_W
    if local:
        (x_ref, xl_ref, xr_ref, cos_ref, cosl_ref, cosr_ref, sin_ref, sinl_ref, sinr_ref, bias_ref,
         kc_ref, vc_ref, hb_ref, mods_ref, win_ref, sink_ref,
         dww_ref, dwb_ref, clg_ref, clb_ref,
         lcw_ref, lcb_ref, wax_ref, bax_ref, lam_ref, seed_ref,
         y_ref, st_ref,
         hext_ref, kext_ref, vext_ref, yatt_ref, cbuf, yext, wext, obuf, lbuf_ref, carry_ref,
         abuf, ubuf, hbuf) = refs
        halo = BLOCK
    else:
        (x_ref, hb_ref, mods_ref, win_ref, sink_ref,
         dww_ref, dwb_ref, clg_ref, clb_ref,
         lcw_ref, lcb_ref, wax_ref, bax_ref, lam_ref, seed_ref,
         y_ref, st_ref, ko_ref, vo_ref,
         hext_ref, kext_ref, vext_ref, yatt_ref, cbuf, yext, wext, obuf, lbuf_ref, carry_ref,
         abuf, ubuf, hbuf) = refs
        halo = 0
    i = pl.program_id(1)
    mod = _mod_row(mods_ref, pl.program_id(0) if mod_row is None else mod_row)
    w_in = lambda c0, c1: win_ref[0, :, c0:c1]

    @pl.when(i == 0)
    def _():
        carry_ref[...] = seed_ref[0]

    hext_ref[halo:halo + T, :] = _ln_mod(x_ref[0], mod)
    if local:
        hext_ref[0:halo, :] = _ln_mod(xl_ref[0], mod)
        hext_ref[halo + T:halo + T + halo, :] = _ln_mod(xr_ref[0], mod)
    hc = hext_ref[halo:halo + T, :]

    state = {}
    if local:
        lane = lax.broadcasted_iota(jnp.int32, (1, V7X_LANES), 1)
        even = (lane & (HEAD_DIM // 4)) == 0

    def with_ones(v):
        ones = jnp.ones((v.shape[0], HEAD_DIM), F32)
        return jnp.concatenate([v[:, :HEAD_DIM], ones, v[:, HEAD_DIM:], ones], axis=1).astype(BF16)

    def proj_kv():
        w_kv = w_in(COL_K, COL_GATT)
        if local:
            parts = ((0, halo, cosl_ref, sinl_ref), (halo, T, cos_ref, sin_ref),
                     (halo + T, halo, cosr_ref, sinr_ref))
            for r0, n, c_ref, s_ref in parts:
                kv = _dot(hext_ref[r0:r0 + n, :], w_kv)
                kext_ref[r0:r0 + n, :] = _rope(kv[:, :KV_W], c_ref[...], s_ref[...], even).astype(BF16)
                vext_ref[r0:r0 + n, :] = with_ones(kv[:, KV_W:])
        else:
            kv = _dot(hc, w_kv)
            kext_ref[...] = kv[:, :KV_W].astype(BF16)
            vext_ref[...] = with_ones(kv[:, KV_W:])
            ko_ref[0] = kext_ref[...]
            vo_ref[0] = vext_ref[...]

    def proj_q():
        q = _dot(hc, w_in(COL_Q, COL_K))
        if local:
            q = _rope(q, cos_ref[...], sin_ref[...], even)
        state["q"] = q * (ATTN_SCALE * LOG2_E)

    def attention(jb, hk):
        q = state["q"]
        r0 = jb * BLOCK
        heads = [GROUP * hk + g for g in range(GROUP)]
        q4 = jnp.concatenate(
            [q[r0:r0 + BLOCK, h * HEAD_DIM:(h + 1) * HEAD_DIM] for h in heads], axis=0).astype(BF16)
        sink = LOG2_E * jnp.concatenate(
            [jnp.broadcast_to(sink_ref[0, h:h + 1, 0:1], (BLOCK, 1)) for h in heads], axis=0)
        ks = slice(hk * HEAD_DIM, (hk + 1) * HEAD_DIM)
        vs = slice(hk * V7X_LANES, (hk + 1) * V7X_LANES)
        if local:
            k_c, v_c = kc_ref[0, :, ks], vc_ref[0, :, vs]
        else:
            k_c, v_c = kext_ref[:, ks], vext_ref[:, vs]
        s_ctx = _dot_nt(q4, k_c)
        m = jnp.maximum(jnp.max(s_ctx, axis=-1, keepdims=True), sink)
        if local:
            pos0 = i * T + r0
            bidx = jnp.where(pos0 == 0, 1, jnp.where(pos0 == S - BLOCK, 2, 0))
            bias = bias_ref[bidx]
            k_w = kext_ref[r0:r0 + 3 * BLOCK, ks]
            v_w = vext_ref[r0:r0 + 3 * BLOCK, vs]
            s_loc = _dot_nt(q4, k_w)
            s_loc = jnp.concatenate(
                [s_loc[g * BLOCK:(g + 1) * BLOCK] + bias for g in range(GROUP)], axis=0)
            m = jnp.maximum(m, jnp.max(s_loc, axis=-1, keepdims=True))
        o = _dot(jnp.exp2(s_ctx - m).astype(BF16), v_c)
        if local:
            o = o + _dot(jnp.exp2(s_loc - m).astype(BF16), v_w)
        den = pltpu.roll(o, HEAD_DIM, axis=1) + jnp.exp2(sink - m)
        o = o / den
        for g, h in enumerate(heads):
            yatt_ref[r0:r0 + BLOCK, h * HEAD_DIM:(h + 1) * HEAD_DIM] = o[g * BLOCK:(g + 1) * BLOCK, :HEAD_DIM]

    n_ext = T + 2 * CONV_HALO
    zeros = jnp.zeros((CONV_HALO, W), F32)
    h_wide = hext_ref[halo - CONV_HALO:halo + T + CONV_HALO, :] if local else hc
    if local:
        pos = i * T - CONV_HALO + lax.broadcasted_iota(jnp.int32, (n_ext, 1), 0)
        inside = (pos >= 0) & (pos < S)
    conv_tasks, conv_finish = _dwconv_tasks(cbuf, yext, wext, obuf, dww_ref, dwb_ref, T=T, K=CONV_K, chunks=2)
    lru_tasks, lru_finish = [], []

    def proj_glu():
        ag = _dot(h_wide, w_in(COL_UCONV, COL_GCONV))
        glu = ag[:, 0:W] * _sigmoid(ag[:, W:2 * W])
        if local:
            glu = jnp.where(inside, glu, 0.0)
            parts = (glu[CONV_HALO:CONV_HALO + T], glu[CONV_HALO + T:n_ext], glu[0:CONV_HALO])
        else:
            parts = (glu, zeros, zeros)
        for r0, part in zip((0, T, T + CONV_HALO), parts):
            for c, slab in enumerate(_lane_slabs(part)):
                cbuf[c, r0:r0 + part.shape[0], :] = slab
        side.append(conv_tasks)

    def lru_front():
        xc = jnp.broadcast_to(lcb_ref[0], (T, W))
        off = CONV_HALO - (LRU_CONV_K - 1)
        for k in range(LRU_CONV_K):
            xc = xc + lcw_ref[0, k:k + 1, :] * lbuf_ref[off + k:off + k + T, :]
        a, u = _rglru_gates(xc, wax_ref, bax_ref, lam_ref)
        tasks, finish = _scan_tasks(a, u, carry_ref[...], abuf, ubuf, hbuf, T=T, reverse=False)
        lru_tasks.extend(tasks)
        lru_finish.append(finish)

    def proj_gate_lru():
        gx = _dot(h_wide, w_in(COL_GCONV, COL_GLRU))
        if local:
            lbuf_ref[...] = jnp.where(inside, gx[:, W:2 * W], 0.0)
            state["g_conv"] = gx[CONV_HALO:CONV_HALO + T, 0:W]
        else:
            lbuf_ref[0:CONV_HALO, :] = zeros
            lbuf_ref[CONV_HALO + T:n_ext, :] = zeros
            lbuf_ref[CONV_HALO:CONV_HALO + T, :] = gx[:, W:2 * W]
            state["g_conv"] = gx[:, 0:W]
        lru_tasks.append(lru_front)
        side.append(lru_tasks)

    side = []
    for proj in (proj_kv, proj_q, proj_glu, proj_gate_lru):
        proj()
    blocks = [(jb, hk) for jb in range(T // BLOCK) for hk in range(N_KV_HEADS)]
    for n, (jb, hk) in enumerate(blocks):
        attention(jb, hk)
        left = len(blocks) - n
        for queue in side:
            for _ in range(-(-len(queue) // left)):
                queue.pop(0)()
    assert not conv_tasks and not lru_tasks
    g_conv = state["g_conv"]

    g_att = _dot(hc, w_in(COL_GATT, COL_UCONV))
    y_ref[0, :, 0:W] = (yatt_ref[...] * _silu(g_att)).astype(BF16)

    y_conv = _silu(_layer_norm(conv_finish()) * clg_ref[0] + clb_ref[0])
    y_ref[0, :, W:2 * W] = (y_conv * _silu(g_conv)).astype(BF16)

    hf, final = lru_finish[0]()
    carry_ref[...] = final
    st_ref[0] = final
    g_lru = _dot(hc, w_in(COL_GLRU, IN_COLS))
    y_ref[0, :, 2 * W:3 * W] = ((hf + hb_ref[0]) * _silu(g_lru)).astype(BF16)


def _mix_call(x, hb, mods, P, seed, *, l, T, local, mod_row, rope=None, bias=None, kc=None, vc=None):
    B, S, _ = x.shape
    nT = S // T
    W = BRANCH_W
    tile = lambda b, i: (b, i, 0)
    batch = lambda b, i: (b, 0, 0)
    lay = lambda b, i: (l, 0, 0)
    dirn = lambda b, i: (2 * l, 0, 0)
    in_specs, args = [pl.BlockSpec((1, T, D_MODEL), tile)], [x]
    if local:
        nb = T // BLOCK
        last = S // BLOCK - 1
        left = lambda b, i: (b, jnp.maximum(i * nb - 1, 0), 0)
        right = lambda b, i: (b, jnp.minimum((i + 1) * nb, last), 0)
        in_specs += [pl.BlockSpec((1, BLOCK, D_MODEL), left), pl.BlockSpec((1, BLOCK, D_MODEL), right)]
        args += [x, x]
        cos, sin = rope
        for tab in (cos, sin):
            in_specs += [
                pl.BlockSpec((T, V7X_LANES), lambda b, i: (i, 0)),
                pl.BlockSpec((BLOCK, V7X_LANES), lambda b, i: (jnp.maximum(i * nb - 1, 0), 0)),
                pl.BlockSpec((BLOCK, V7X_LANES), lambda b, i: (jnp.minimum((i + 1) * nb, last), 0)),
            ]
            args += [tab, tab, tab]
        C = kc.shape[1]
        in_specs += [
            pl.BlockSpec((3, BLOCK, 3 * BLOCK), lambda b, i: (0, 0, 0)),
            pl.BlockSpec((1, C, KV_W), batch),
            pl.BlockSpec((1, C, V_EXT_W), batch),
        ]
        args += [bias, kc, vc]
    in_specs += [
        pl.BlockSpec((1, T, W), tile),
        pl.BlockSpec((1, V7X_SUBLANES, 3 * D_MODEL), lay),
        pl.BlockSpec((1, D_MODEL, IN_COLS), lay),
        pl.BlockSpec((1, N_HEADS, V7X_LANES), lay),
        pl.BlockSpec((1, CONV_K, W), lay),
        pl.BlockSpec((1, 1, W), lay),
        pl.BlockSpec((1, 1, W), lay),
        pl.BlockSpec((1, 1, W), lay),
        pl.BlockSpec((1, LRU_CONV_K, W), dirn),
        pl.BlockSpec((1, 1, W), dirn),
        pl.BlockSpec((1, 2 * LRU_TILES, MXU_TILE, MXU_TILE), lambda b, i, d=dirn: d(b, i) + (0,)),
        pl.BlockSpec((1, 1, 2 * W), dirn),
        pl.BlockSpec((1, 1, W), dirn),
        pl.BlockSpec((1, 1, W), batch),
    ]
    args += [hb, mods, P["w_in"], P["sink"], P["dw_w"], P["dw_b"], P["cl_g"], P["cl_b"],
             P["lru_cw"], P["lru_cb"], P["lru_wax"], P["lru_bax"], P["lru_lam"], seed]
    out_specs = [pl.BlockSpec((1, T, 3 * W), tile), pl.BlockSpec((1, 1, W), batch)]
    out_shape = [jax.ShapeDtypeStruct((B, S, 3 * W), BF16), jax.ShapeDtypeStruct((B, 1, W), F32)]
    if not local:
        out_specs += [pl.BlockSpec((1, T, KV_W), tile), pl.BlockSpec((1, T, V_EXT_W), tile)]
        out_shape += [jax.ShapeDtypeStruct((B, S, KV_W), BF16), jax.ShapeDtypeStruct((B, S, V_EXT_W), BF16)]
    halo = BLOCK if local else 0
    n_ext = T + 2 * CONV_HALO
    L = _conv_seg_len(T)
    scratch = [
        pltpu.VMEM((T + 2 * halo, D_MODEL), BF16),
        pltpu.VMEM((T + 2 * halo, KV_W), BF16),
        pltpu.VMEM((T + 2 * halo, V_EXT_W), BF16),
        pltpu.VMEM((T, W), F32),
        pltpu.VMEM((N_SLAB, n_ext, V7X_LANES), F32),
        pltpu.VMEM((N_SLAB, L + 2 * (CONV_K // 2), CONV_SEGMENTS, V7X_LANES), BF16),
        pltpu.VMEM((N_SLAB, CONV_K, CONV_SEGMENTS, V7X_LANES), BF16),
        pltpu.VMEM((N_SLAB, n_ext, V7X_LANES), F32),
        pltpu.VMEM((n_ext, W), F32),
        pltpu.VMEM((1, W), F32),
    ] + _lru_scratch(T)
    return pl.pallas_call(
        functools.partial(_mix_kernel, T=T, S=S, local=local, mod_row=mod_row),
        grid=(B, nT),
        in_specs=in_specs,
        out_specs=out_specs,
        out_shape=out_shape,
        scratch_shapes=scratch,
        compiler_params=pltpu.CompilerParams(
            dimension_semantics=("arbitrary", "arbitrary"), vmem_limit_bytes=VMEM_LIMIT_BYTES),
        name="mix_local" if local else "mix_ctx",
    )(*args)


def _merge_kernel(x_ref, y_ref, mods_ref, wgate_ref, bgate_ref, wbr_ref, wout_ref, lng_ref, lnb_ref, o_ref, *,
                  mod_row):
    x = x_ref[0]
    mod = _mod_row(mods_ref, pl.program_id(0) if mod_row is None else mod_row)
    h = _ln_mod(x, mod)
    m = None
    for n in range(N_BRANCH):
        cols = slice(n * D_MODEL, (n + 1) * D_MODEL)
        g = _sigmoid(_dot(h, wgate_ref[0, :, cols]) + bgate_ref[0, :, cols])
        p = _dot(y_ref[0, :, n * BRANCH_W:(n + 1) * BRANCH_W], wbr_ref[0, n])
        m = p * g if m is None else m + p * g
    out = _dot(m.astype(BF16), wout_ref[0])
    gate = mod[:, 2 * D_MODEL:3 * D_MODEL]
    z = DEEPNORM_ALPHA * x + gate * out
    o_ref[0] = _layer_norm(z) * lng_ref[0] + lnb_ref[0]


def _merge_call(x, y, mods, P, *, l, T, mod_row):
    B, S, _ = x.shape
    nT = S // T
    tile = lambda b, i: (b, i, 0)
    lay = lambda b, i: (l, 0, 0)
    return pl.pallas_call(
        functools.partial(_merge_kernel, mod_row=mod_row),
        grid=(B, nT),
        in_specs=[
            pl.BlockSpec((1, T, D_MODEL), tile),
            pl.BlockSpec((1, T, 3 * BRANCH_W), tile),
            pl.BlockSpec((1, V7X_SUBLANES, 3 * D_MODEL), lay),
            pl.BlockSpec((1, D_MODEL, 3 * D_MODEL), lay, pipeline_mode=pl.Buffered(1)),
            pl.BlockSpec((1, 1, 3 * D_MODEL), lay),
            pl.BlockSpec((1, N_BRANCH, BRANCH_W, D_MODEL), lambda b, i: (l, 0, 0, 0), pipeline_mode=pl.Buffered(1)),
            pl.BlockSpec((1, D_MODEL, D_MODEL), lay, pipeline_mode=pl.Buffered(1)),
            pl.BlockSpec((1, 1, D_MODEL), lay),
            pl.BlockSpec((1, 1, D_MODEL), lay),
        ],
        out_specs=pl.BlockSpec((1, T, D_MODEL), tile),
        out_shape=jax.ShapeDtypeStruct((B, S, D_MODEL), F32),
        compiler_params=pltpu.CompilerParams(
            dimension_semantics=("arbitrary", "arbitrary"), vmem_limit_bytes=VMEM_LIMIT_BYTES),
        name="merge",
    )(x, y, mods, P["w_gate"], P["b_gate"], P["w_branch"], P["w_out"], P["ln_g"], P["ln_b"])


def _rope_tables(S):
    quarter = HEAD_DIM // 4
    rows = S // GRID_W
    inv = ROPE_BASE ** (-jnp.arange(quarter, dtype=F32) / quarter)
    ang_r = jnp.arange(rows, dtype=F32)[:, None] * inv
    ang_c = jnp.arange(GRID_W, dtype=F32)[:, None] * inv

    def table(fn, signs):
        r = jnp.broadcast_to(fn(ang_r)[:, None, :], (rows, GRID_W, quarter))
        c = jnp.broadcast_to(fn(ang_c)[None, :, :], (rows, GRID_W, quarter))
        head = [signs[0] * r, signs[1] * r, signs[0] * c, signs[1] * c]
        return jnp.concatenate(head * (V7X_LANES // HEAD_DIM), axis=-1).reshape(S, V7X_LANES)

    return table(jnp.cos, (1.0, 1.0)), table(jnp.sin, (-1.0, 1.0))


def _window_bias():
    r = np.arange(BLOCK)[:, None]
    j = np.arange(3 * BLOCK)[None, :]
    band = np.abs(j - BLOCK - r) <= WINDOW
    first = band & (j >= BLOCK)
    last = band & (j < 2 * BLOCK)
    return jnp.asarray(np.where(np.stack([band, first, last]), 0.0, -np.inf), dtype=F32)


def _gate_tiles(w):
    blk = w.shape[-1]
    per = MXU_TILE // blk
    wt = w.astype(BF16).reshape(w.shape[:2] + (LRU_TILES, per, blk, blk))
    lead = ((0, 0),) * 3
    return sum(jnp.pad(wt[:, :, :, n], lead + ((n * blk, MXU_TILE - (n + 1) * blk),) * 2) for n in range(per))


def _pack_params(w_in, attn_sink, conv_dw_w, conv_dw_b, conv_ln_g, conv_ln_b, lru_conv_w, lru_conv_b,
                 lru_w_a, lru_b_a, lru_w_x, lru_b_x, lru_lambda, w_branch, w_gate, b_gate, w_out, ln_g, ln_b):
    depth = w_in.shape[0]
    W = BRANCH_W
    rows = lambda v: v.reshape(-1, 1, v.shape[-1])
    return {
        "w_in": w_in.astype(BF16),
        "sink": jnp.broadcast_to(attn_sink[:, :, None], (depth, N_HEADS, V7X_LANES)),
        "dw_w": conv_dw_w, "dw_b": rows(conv_dw_b), "cl_g": rows(conv_ln_g), "cl_b": rows(conv_ln_b),
        "lru_cw": lru_conv_w.reshape(2 * depth, LRU_CONV_K, W),
        "lru_cb": rows(lru_conv_b),
        "lru_wax": jnp.concatenate([_gate_tiles(lru_w_a), _gate_tiles(lru_w_x)], axis=2)
                      .reshape(2 * depth, 2 * LRU_TILES, MXU_TILE, MXU_TILE),
        "lru_bax": rows(jnp.concatenate([lru_b_a, lru_b_x], axis=-1)),
        "lru_lam": rows(lru_lambda),
        "w_gate": w_gate.astype(BF16), "b_gate": rows(b_gate),
        "w_branch": w_branch.astype(BF16), "w_out": w_out.astype(BF16),
        "ln_g": rows(ln_g), "ln_b": rows(ln_b),
    }


TILE_T = 512
LRU_TILE_T = 1024
MERGE_TILE_T = 1024


def kernel(x, c, ctx, c_ctx, w_ada, b_ada, w_in, attn_sink, conv_dw_w, conv_dw_b, conv_ln_g, conv_ln_b,
           lru_conv_w, lru_conv_b, lru_w_a, lru_b_a, lru_w_x, lru_b_x, lru_lambda,
           w_branch, w_gate, b_gate, w_out, ln_g, ln_b):
    B, S, D = x.shape
    C = ctx.shape[1]
    depth = w_ada.shape[0]
    assert D == D_MODEL and S % TILE_T == 0 and S % LRU_TILE_T == 0 and S >= 2 * BLOCK and B + 1 <= V7X_SUBLANES

    c_rows = jnp.concatenate([c, c_ctx[None, :], jnp.zeros((V7X_SUBLANES - B - 1, D), F32)], axis=0)
    mods = _mods_call(c_rows, w_ada, b_ada)
    P = _pack_params(w_in, attn_sink, conv_dw_w, conv_dw_b, conv_ln_g, conv_ln_b, lru_conv_w, lru_conv_b,
                     lru_w_a, lru_b_a, lru_w_x, lru_b_x, lru_lambda, w_branch, w_gate, b_gate, w_out, ln_g, ln_b)
    rope = _rope_tables(S)
    bias = _window_bias()
    zero_seed = jnp.zeros((B, 1, BRANCH_W), F32)

    for l in range(depth):
        hb_c, seed_b = _lru_bwd_call(ctx, mods, P, zero_seed, l=l, T=C, mod_row=B)
        y_c, seed_f, kc, vc = _mix_call(ctx, hb_c, mods, P, zero_seed, l=l, T=C, local=False, mod_row=B)

        hb = _lru_bwd_call(x, mods, P, seed_b, l=l, T=LRU_TILE_T, mod_row=None)[0]
        y = _mix_call(x, hb, mods, P, seed_f, l=l, T=TILE_T, local=True, mod_row=None,
                      rope=rope, bias=bias, kc=kc, vc=vc)[0]
        x = _merge_call(x, y, mods, P, l=l, T=MERGE_TILE_T, mod_row=None)
        if l != depth - 1:
            ctx = _merge_call(ctx, y_c, mods, P, l=l, T=C, mod_row=B)
    return x
```
